```python
import math
import jax, jax.numpy as jnp
from jax import lax
import numpy as np


D_MODEL = 2048
BATCH = 1
SEQ = 16384
DEPTH = 2

GRID_W = 64
CTX_LEN = 256
HEAD_DIM = 128
DA_HEADS = 4
NA_HEADS = 8
DA_WIDTH = DA_HEADS * 2 * HEAD_DIM
NA_WIDTH = NA_HEADS * HEAD_DIM
MIX_DIM = DA_WIDTH + NA_WIDTH
IN_DIM = 3 * DA_WIDTH + 3 * NA_WIDTH
IN_SPLITS = [DA_WIDTH, 2 * DA_WIDTH, 3 * DA_WIDTH, 3 * DA_WIDTH + NA_WIDTH, 3 * DA_WIDTH + 2 * NA_WIDTH]
NA_KH = 8
NA_KW = 16
D_FF = 5632
CONV_W = 3
ROPE_THETA = 10000.0
Q_BLOCK = 128
LN_EPS = 1e-5
ATTN_SCALE = HEAD_DIM ** -0.5
N_MOD = 6

kernel_name = 'hybrid_diffattn_natten_convffn_deepnorm'


def _layernorm(x, g, b):
    xf = x.astype(jnp.float32)
    mu = jnp.mean(xf, axis=-1, keepdims=True)
    var = jnp.mean(jnp.square(xf - mu), axis=-1, keepdims=True)
    y = (xf - mu) * lax.rsqrt(var + LN_EPS)
    return (y * g.astype(jnp.float32) + b.astype(jnp.float32)).astype(x.dtype)


def _rmsnorm(x, g):
    xf = x.astype(jnp.float32)
    y = xf * lax.rsqrt(jnp.mean(jnp.square(xf), axis=-1, keepdims=True) + LN_EPS)
    return (y * g.astype(jnp.float32)).astype(x.dtype)


def _to_blocks(t, qb):
    b, l = t.shape[:2]
    return jnp.moveaxis(t.reshape((b, l // qb, qb) + t.shape[2:]), 1, 0)


def _from_blocks(t):
    nb, b, qb = t.shape[:3]
    return jnp.moveaxis(t, 0, 1).reshape((b, nb * qb) + t.shape[3:])


def _axial_rope_tables(L):
    t = jnp.arange(L, dtype=jnp.int32)
    pos_r = (t // GRID_W).astype(jnp.float32)
    pos_c = (t % GRID_W).astype(jnp.float32)
    half = HEAD_DIM // 2
    inv_freq = 1.0 / (ROPE_THETA ** (jnp.arange(0, half, 2, dtype=jnp.float32) / half))
    ar = pos_r[:, None] * inv_freq[None, :]
    ac = pos_c[:, None] * inv_freq[None, :]
    ang = jnp.concatenate([ar, ar, ac, ac], axis=-1)
    return jnp.cos(ang), jnp.sin(ang)


def _apply_rope(x, cos, sin):
    cos = cos[:, None, None, :].astype(x.dtype)
    sin = sin[:, None, None, :].astype(x.dtype)
    a1, a2, b1, b2 = jnp.split(x, 4, axis=-1)
    rot = jnp.concatenate([-a2, a1, -b2, b1], axis=-1)
    return x * cos + rot * sin


def _na_neighbours(L):
    rows = L // GRID_W
    kh = min(NA_KH, rows)
    t = jnp.arange(L, dtype=jnp.int32)
    r = t // GRID_W
    col = t % GRID_W
    rs = jnp.clip(r - kh // 2, 0, rows - kh)
    cs = jnp.clip(col - NA_KW // 2, 0, GRID_W - NA_KW)
    kr = rs[:, None, None] + jnp.arange(kh, dtype=jnp.int32)[None, :, None]
    kc = cs[:, None, None] + jnp.arange(NA_KW, dtype=jnp.int32)[None, None, :]
    shape = (L, kh, NA_KW)
    idx = jnp.broadcast_to(kr * GRID_W + kc, shape).reshape(L, kh * NA_KW)
    br = jnp.broadcast_to(kr - r[:, None, None] + (NA_KH - 1), shape).reshape(L, kh * NA_KW)
    bc = jnp.broadcast_to(kc - col[:, None, None] + (NA_KW - 1), shape).reshape(L, kh * NA_KW)
    return idx, br, bc


def _project(h, w_in):
    b, l, _ = h.shape
    q_da, k_da, v_da, q_na, k_na, v_na = jnp.split(h @ w_in, IN_SPLITS, axis=-1)
    return (q_da.reshape(b, l, DA_HEADS, 2, HEAD_DIM),
            k_da.reshape(b, l, DA_HEADS, 2, HEAD_DIM),
            v_da.reshape(b, l, DA_HEADS, 2 * HEAD_DIM),
            q_na.reshape(b, l, NA_HEADS, HEAD_DIM),
            k_na.reshape(b, l, NA_HEADS, HEAD_DIM),
            v_na.reshape(b, l, NA_HEADS, HEAD_DIM))


def _diff_weights(q, k, lam):
    s = jnp.einsum('bqhcd,bkhcd->bhcqk', q, k).astype(jnp.float32) * ATTN_SCALE
    p = jax.nn.softmax(s, axis=-1)
    return p[:, :, 0] - lam * p[:, :, 1]


def _diff_attention_latent(q, k_all, v_all, lam):
    def block(qb):
        w = _diff_weights(qb, k_all, lam).astype(v_all.dtype)
        return jnp.einsum('bhqk,bkhe->bqhe', w, v_all)
    return _from_blocks(lax.map(block, _to_blocks(q, Q_BLOCK)))


def _na_latent(q, k_lat, v_lat, k_ctx, v_ctx, rpb, idx, br, bc):
    nk = idx.shape[-1]
    nb = idx.shape[0] // Q_BLOCK

    def block(args):
        qb, ib, rb, cb = args
        kg = jnp.take(k_lat, ib, axis=1)
        vg = jnp.take(v_lat, ib, axis=1)
        s_loc = jnp.einsum('bqhd,bqnhd->bhqn', qb, kg).astype(jnp.float32) * ATTN_SCALE
        s_loc = s_loc + rpb[:, rb, cb].astype(jnp.float32)[None]
        s_ctx = jnp.einsum('bqhd,bkhd->bhqk', qb, k_ctx).astype(jnp.float32) * ATTN_SCALE
        p = jax.nn.softmax(jnp.concatenate([s_loc, s_ctx], axis=-1), axis=-1).astype(qb.dtype)
        return (jnp.einsum('bhqn,bqnhd->bqhd', p[..., :nk], vg)
                + jnp.einsum('bhqk,bkhd->bqhd', p[..., nk:], v_ctx))

    xs = (_to_blocks(q, Q_BLOCK), idx.reshape(nb, Q_BLOCK, nk),
          br.reshape(nb, Q_BLOCK, nk), bc.reshape(nb, Q_BLOCK, nk))
    return _from_blocks(lax.map(block, xs))


def _dense_attention(q, k, v):
    s = jnp.einsum('bqhd,bkhd->bhqk', q, k).astype(jnp.float32) * ATTN_SCALE
    p = jax.nn.softmax(s, axis=-1).astype(v.dtype)
    return jnp.einsum('bhqk,bkhd->bqhd', p, v)


def _mixer(h, hc, w_in, lam_vec, subln_g, rpb, w_o, lambda_init, cos, sin, nbr, with_ctx_out):
    b, l, _ = h.shape
    qd, kd, vd, qn, kn, vn = _project(h, w_in)
    qdc, kdc, vdc, qnc, knc, vnc = _project(hc, w_in)
    lv = lam_vec.astype(jnp.float32)
    lam = jnp.exp(jnp.sum(lv[0] * lv[1])) - jnp.exp(jnp.sum(lv[2] * lv[3])) + lambda_init
    qd = _apply_rope(qd, cos, sin)
    kd = _apply_rope(kd, cos, sin)
    k_all = jnp.concatenate([kdc, kd], axis=1)
    v_all = jnp.concatenate([vdc, vd], axis=1)
    od = _diff_attention_latent(qd, k_all, v_all, lam)
    od = (_rmsnorm(od, subln_g) * (1.0 - lambda_init)).reshape(b, l, DA_WIDTH)
    on = _na_latent(qn, kn, vn, knc, vnc, rpb, *nbr).reshape(b, l, NA_WIDTH)
    y = jnp.concatenate([od, on], axis=-1) @ w_o
    if not with_ctx_out:
        return y, None
    lc = hc.shape[1]
    wdc = _diff_weights(qdc, kdc, lam).astype(vdc.dtype)
    odc = jnp.einsum('bhqk,bkhe->bqhe', wdc, vdc)
    odc = (_rmsnorm(odc, subln_g) * (1.0 - lambda_init)).reshape(b, lc, DA_WIDTH)
    onc = _dense_attention(qnc, knc, vnc).reshape(b, lc, NA_WIDTH)
    yc = jnp.concatenate([odc, onc], axis=-1) @ w_o
    return y, yc


def _dwconv3(x, w, bias):
    xp = jnp.pad(x, ((0, 0), (1, 1), (0, 0)))
    return xp[:, :-2] * w[0] + xp[:, 1:-1] * w[1] + xp[:, 2:] * w[2] + bias


def _conv_ffn(h, w_up, conv_w, conv_b, w_down):
    g, u = jnp.split(h @ w_up, 2, axis=-1)
    g = _dwconv3(g, conv_w, conv_b)
    return (jax.nn.silu(g) * u) @ w_down


def setup_inputs(seed: int = 0) -> dict:
    key = jax.random.key(seed)
    ks = jax.random.split(key, 24)
    beta = (8.0 * DEPTH) ** -0.25
    f32 = jnp.float32

    def nrm(k, shape, scale):
        return jax.random.normal(k, shape, f32) * scale

    return {
        'x': nrm(ks[0], (BATCH, SEQ, D_MODEL), 1.0),
        'c': nrm(ks[1], (BATCH, D_MODEL), 1.0),
        'ctx': nrm(ks[2], (BATCH, CTX_LEN, D_MODEL), 1.0),
        'c_ctx': nrm(ks[3], (D_MODEL,), 1.0),
        'w_ada': nrm(ks[4], (DEPTH, D_MODEL, N_MOD * D_MODEL), 0.25 * D_MODEL ** -0.5),
        'b_ada': nrm(ks[5], (DEPTH, N_MOD * D_MODEL), 0.01),
        'w_in': nrm(ks[6], (DEPTH, D_MODEL, IN_DIM), D_MODEL ** -0.5),
        'da_lambda': nrm(ks[7], (DEPTH, 4, HEAD_DIM), 0.1),
        'da_subln': 1.0 + nrm(ks[8], (DEPTH, 2 * HEAD_DIM), 0.02),
        'na_rpb': nrm(ks[9], (DEPTH, NA_HEADS, 2 * NA_KH - 1, 2 * NA_KW - 1), 0.1),
        'w_o': nrm(ks[10], (DEPTH, MIX_DIM, D_MODEL), beta * MIX_DIM ** -0.5),
        'ln1_g': 1.0 + nrm(ks[11], (DEPTH, D_MODEL), 0.02),
        'ln1_b': nrm(ks[12], (DEPTH, D_MODEL), 0.02),
        'w_up': nrm(ks[13], (DEPTH, D_MODEL, 2 * D_FF), D_MODEL ** -0.5),
        'conv_w': nrm(ks[14], (DEPTH, CONV_W, D_FF), CONV_W ** -0.5),
        'conv_b': nrm(ks[15], (DEPTH, D_FF), 0.02),
        'w_down': nrm(ks[16], (DEPTH, D_FF, D_MODEL), beta * D_FF ** -0.5),
        'ln2_g': 1.0 + nrm(ks[17], (DEPTH, D_MODEL), 0.02),
        'ln2_b': nrm(ks[18], (DEPTH, D_MODEL), 0.02),
    }


def reference(x, c, ctx, c_ctx, w_ada, b_ada, w_in, da_lambda, da_subln, na_rpb, w_o,
              ln1_g, ln1_b, w_up, conv_w, conv_b, w_down, ln2_g, ln2_b):
    L = x.shape[1]
    alpha = (2.0 * DEPTH) ** 0.25
    cos, sin = _axial_rope_tables(L)
    nbr = _na_neighbours(L)
    xc = ctx
    for l in range(DEPTH):
        last = l == DEPTH - 1
        lambda_init = 0.8 - 0.6 * math.exp(-0.3 * l)
        mod = (jax.nn.silu(c) @ w_ada[l] + b_ada[l])[:, None, :]
        mod_c = jax.nn.silu(c_ctx) @ w_ada[l] + b_ada[l]
        sh_a, sc_a, g_a, sh_m, sc_m, g_m = jnp.split(mod, N_MOD, axis=-1)
        shc_a, scc_a, gc_a, shc_m, scc_m, gc_m = jnp.split(mod_c, N_MOD, axis=-1)
        y, yc = _mixer(x * (1.0 + sc_a) + sh_a, xc * (1.0 + scc_a) + shc_a,
                       w_in[l], da_lambda[l], da_subln[l], na_rpb[l], w_o[l],
                       lambda_init, cos, sin, nbr, not last)
        x = _layernorm(alpha * x + g_a * y, ln1_g[l], ln1_b[l])
        f = _conv_ffn(x * (1.0 + sc_m) + sh_m, w_up[l], conv_w[l], conv_b[l], w_down[l])
        x = _layernorm(alpha * x + g_m * f, ln2_g[l], ln2_b[l])
        if not last:
            xc = _layernorm(alpha * xc + gc_a * yc, ln1_g[l], ln1_b[l])
            fc = _conv_ffn(xc * (1.0 + scc_m) + shc_m, w_up[l], conv_w[l], conv_b[l], w_down[l])
            xc = _layernorm(alpha * xc + gc_m * fc, ln2_g[l], ln2_b[l])
    return x
```

```python
import functools
import math

import numpy as np
import jax
import jax.numpy as jnp
from jax import lax
from jax.experimental import pallas as pl
from jax.experimental.pallas import tpu as pltpu

D_MODEL = 2048
DEPTH = 2
GRID_W = 64
HEAD_DIM = 128
DA_HEADS = 4
NA_HEADS = 8
DA_WIDTH = DA_HEADS * 2 * HEAD_DIM
NA_WIDTH = NA_HEADS * HEAD_DIM
IN_DIM = 3 * DA_WIDTH + 3 * NA_WIDTH
NA_KH = 8
NA_KW = 16
D_FF = 5632
ROPE_THETA = 10000.0
LN_EPS = 1e-5
N_MOD = 6
ALPHA = (2.0 * DEPTH) ** 0.25
LOG2E = math.log2(math.e)
Q_SCALE = (HEAD_DIM ** -0.5) * LOG2E
MASK_VALUE = -1e30

F32 = jnp.float32
BF16 = jnp.bfloat16

VMEM_LIMIT = 56 * 1024 * 1024

QD_COL, KD_COL, VD_COL = 0, DA_WIDTH // 128, 2 * DA_WIDTH // 128
QN_COL = 3 * DA_WIDTH // 128
KN_COL = QN_COL + NA_WIDTH // 128
VN_COL = KN_COL + NA_WIDTH // 128


def _params(semantics):
    return pltpu.CompilerParams(dimension_semantics=semantics, vmem_limit_bytes=VMEM_LIMIT)


def _dot(a, b):
    return jnp.dot(a, b, preferred_element_type=F32)


def _dot_nt(a, b):
    return lax.dot_general(a, b, (((1,), (1,)), ((), ())), preferred_element_type=F32)


def _layernorm_rows(z, g, b):
    mu = jnp.mean(z, axis=-1, keepdims=True)
    zc = z - mu
    var = jnp.mean(zc * zc, axis=-1, keepdims=True)
    return zc * lax.rsqrt(var + LN_EPS) * g + b


ADA_TN = 1024


def _ada_kernel(c_ref, w_ref, b_ref, o_ref):
    a = c_ref[...]
    s = a / (1.0 + jnp.exp(-a))
    o_ref[...] = jnp.dot(s, w_ref[...], preferred_element_type=F32,
                         precision=lax.Precision.HIGHEST) + b_ref[...]


def _ada_mod(cc, w_ada, b_ada):
    nmod = N_MOD * D_MODEL
    return pl.pallas_call(
        _ada_kernel,
        grid=(DEPTH, nmod // ADA_TN),
        in_specs=[
            pl.BlockSpec((8, D_MODEL), lambda l, j: (0, 0)),
            pl.BlockSpec((None, D_MODEL, ADA_TN), lambda l, j: (l, 0, j)),
            pl.BlockSpec((None, 1, ADA_TN), lambda l, j: (l, 0, j)),
        ],
        out_specs=pl.BlockSpec((None, 8, ADA_TN), lambda l, j: (l, 0, j)),
        out_shape=jax.ShapeDtypeStruct((DEPTH, 8, nmod), F32),
        compiler_params=_params(("parallel", "parallel")),
        name="ada_mod",
    )(cc, w_ada, b_ada.reshape(DEPTH, 1, nmod))


PROJ_TN = 1024


def _in_proj_kernel(x_ref, sc_ref, sh_ref, w_ref, cos_ref, sa_ref, sb_ref, o_ref, h_ref):
    j = pl.program_id(1)

    @pl.when(j == 0)
    def _():
        h_ref[...] = (x_ref[...] * (1.0 + sc_ref[...]) + sh_ref[...]).astype(BF16)

    acc = _dot(h_ref[...], w_ref[...])

    @pl.when(j <= 1)
    def _():
        scale = jnp.where(j == 0, Q_SCALE, 1.0).astype(F32)
        cos = cos_ref[...] * scale
        sa = sa_ref[...] * scale
        sb = sb_ref[...] * scale
        for k in range(PROJ_TN // HEAD_DIM):
            a = acc[:, k * HEAD_DIM:(k + 1) * HEAD_DIM]
            r = a * cos + pltpu.roll(a, 96, 1) * sa + pltpu.roll(a, 32, 1) * sb
            o_ref[:, k * HEAD_DIM:(k + 1) * HEAD_DIM] = r.astype(BF16)

    @pl.when(j == 3)
    def _():
        o_ref[...] = (acc * Q_SCALE).astype(BF16)

    @pl.when((j == 2) | (j >= 4))
    def _():
        o_ref[...] = acc.astype(BF16)


def _in_proj(x, sc, sh, w, cos, sa, sb, tm):
    m = x.shape[0]
    return pl.pallas_call(
        _in_proj_kernel,
        grid=(m // tm, IN_DIM // PROJ_TN),
        in_specs=[
            pl.BlockSpec((tm, D_MODEL), lambda i, j: (i, 0)),
            pl.BlockSpec((1, D_MODEL), lambda i, j: (0, 0)),
            pl.BlockSpec((1, D_MODEL), lambda i, j: (0, 0)),
            pl.BlockSpec((D_MODEL, PROJ_TN), lambda i, j: (0, j)),
            pl.BlockSpec((tm, HEAD_DIM), lambda i, j: (i, 0)),
            pl.BlockSpec((tm, HEAD_DIM), lambda i, j: (i, 0)),
            pl.BlockSpec((tm, HEAD_DIM), lambda i, j: (i, 0)),
        ],
        out_specs=pl.BlockSpec((tm, PROJ_TN), lambda i, j: (i, j)),
        out_shape=jax.ShapeDtypeStruct((m, IN_DIM), BF16),
        scratch_shapes=[pltpu.VMEM((tm, D_MODEL), BF16)],
        compiler_params=_params(("parallel", "arbitrary")),
        name="in_proj",
    )(x, sc, sh, w, cos, sa, sb)


def _lambda_value(lv, lambda_init):
    a = jnp.sum(lv[0:1, :] * lv[1:2, :], axis=-1, keepdims=True)
    b = jnp.sum(lv[2:3, :] * lv[3:4, :], axis=-1, keepdims=True)
    return jnp.exp(a) - jnp.exp(b) + lambda_init


def _subln(o, g, lambda_init):
    ms = jnp.mean(o * o, axis=-1, keepdims=True)
    return o * lax.rsqrt(ms + LN_EPS) * g * (1.0 - lambda_init)


def _diff_attn_kernel(q_ref, k_ref, v_ref, kc_ref, vc_ref, lam_ref, g_ref, o_ref,
                      acc1_ref, acc2_ref, *, tk, lambda_init):
    nk = k_ref.shape[0] // tk
    qs = (q_ref[:, 0:HEAD_DIM], q_ref[:, HEAD_DIM:2 * HEAD_DIM])
    accs = (acc1_ref, acc2_ref)

    vc = vc_ref[...]
    carry = []
    for c in range(2):
        s = _dot_nt(qs[c], kc_ref[:, c * HEAD_DIM:(c + 1) * HEAD_DIM])
        m = jnp.max(s, axis=-1, keepdims=True)
        p = jnp.exp2(s - m)
        l = jnp.sum(p, axis=-1, keepdims=True)
        accs[c][...] = _dot(p.astype(BF16), vc)
        carry += [m, l]

    def body(i, carry):
        off = pl.multiple_of(i * tk, tk)
        vb = v_ref[pl.ds(off, tk), :]
        out = []
        for c in range(2):
            m, l = carry[2 * c], carry[2 * c + 1]
            s = _dot_nt(qs[c], k_ref[pl.ds(off, tk), c * HEAD_DIM:(c + 1) * HEAD_DIM])
            mn = jnp.maximum(m, jnp.max(s, axis=-1, keepdims=True))
            a = jnp.exp2(m - mn)
            p = jnp.exp2(s - mn)
            l = a * l + jnp.sum(p, axis=-1, keepdims=True)
            accs[c][...] = a * accs[c][...] + _dot(p.astype(BF16), vb)
            out += [mn, l]
        return tuple(out)

    m1, l1, m2, l2 = lax.fori_loop(0, nk, body, tuple(carry))
    lam = _lambda_value(lam_ref[...], lambda_init)
    o = acc1_ref[...] / l1 - lam * (acc2_ref[...] / l2)
    o_ref[...] = _subln(o, g_ref[...], lambda_init).astype(BF16)


def _diff_attn(qkv, qkvc, lam_vec, subln_g, lambda_init, tq, tk):
    l = qkv.shape[0]
    lc = qkvc.shape[0]
    kern = functools.partial(_diff_attn_kernel, tk=tk, lambda_init=lambda_init)
    return pl.pallas_call(
        kern,
        grid=(DA_HEADS, l // tq),
        in_specs=[
            pl.BlockSpec((tq, 256), lambda h, i: (i, QD_COL // 2 + h)),
            pl.BlockSpec((l, 256), lambda h, i: (0, KD_COL // 2 + h)),
            pl.BlockSpec((l, 256), lambda h, i: (0, VD_COL // 2 + h)),
            pl.BlockSpec((lc, 256), lambda h, i: (0, KD_COL // 2 + h)),
            pl.BlockSpec((lc, 256), lambda h, i: (0, VD_COL // 2 + h)),
            pl.BlockSpec((4, HEAD_DIM), lambda h, i: (0, 0)),
            pl.BlockSpec((1, 2 * HEAD_DIM), lambda h, i: (0, 0)),
        ],
        out_specs=pl.BlockSpec((tq, 256), lambda h, i: (i, h)),
        out_shape=jax.ShapeDtypeStruct((l, DA_WIDTH), BF16),
        scratch_shapes=[pltpu.VMEM((tq, 256), F32), pltpu.VMEM((tq, 256), F32)],
        compiler_params=_params(("parallel", "arbitrary")),
        name="diff_attn",
    )(qkv, qkv, qkv, qkvc, qkvc, lam_vec, subln_g)


NA_ROWS = 4
NA_TQ = NA_ROWS * GRID_W
NA_SLAB = NA_ROWS + NA_KH - 1
NA_SK = NA_SLAB * GRID_W


def _na_plan(l):
    rows = l // GRID_W
    kh = min(NA_KH, rows)
    assert kh == NA_KH and rows >= NA_SLAB and rows % NA_ROWS == 0
    nblk = rows // NA_ROWS
    starts, pats, geoms = [], [], []
    for b in range(nblk):
        r0 = b * NA_ROWS
        kstart = int(np.clip(r0 - kh // 2, 0, rows - NA_SLAB))
        rs = [int(np.clip(r0 + qr - kh // 2, 0, rows - kh)) for qr in range(NA_ROWS)]
        geom = (r0 - kstart, tuple(x - kstart for x in rs))
        assert all(0 <= x and x + kh <= NA_SLAB for x in geom[1])
        if geom not in geoms:
            geoms.append(geom)
        starts.append(kstart * GRID_W)
        pats.append(geoms.index(geom))
    return np.asarray(starts, np.int32), np.asarray(pats, np.int32), geoms


def _na_bias_table(rpb, geoms):
    c = np.arange(GRID_W)
    dc = c[None, :] - c[:, None] + (NA_KW - 1)
    cs = np.clip(c - NA_KW // 2, 0, GRID_W - NA_KW)
    col_ok = (c[None, :] >= cs[:, None]) & (c[None, :] < cs[:, None] + NA_KW)
    e_col = np.zeros((2 * NA_KW - 1, GRID_W, GRID_W), np.float32)
    for ci in range(GRID_W):
        for ki in range(GRID_W):
            if col_ok[ci, ki]:
                e_col[dc[ci, ki], ci, ki] = 1.0
    npat = len(geoms)
    e_row = np.zeros((npat, NA_ROWS, NA_SLAB, 2 * NA_KH - 1), np.float32)
    row_ok = np.zeros((npat, NA_ROWS, NA_SLAB), bool)
    for p, (dq, rs) in enumerate(geoms):
        for qr in range(NA_ROWS):
            for j in range(NA_SLAB):
                if rs[qr] <= j < rs[qr] + NA_KH:
                    e_row[p, qr, j, j - (dq + qr) + (NA_KH - 1)] = 1.0
                    row_ok[p, qr, j] = True
    hi = lax.Precision.HIGHEST
    t = jnp.einsum('hrd,dck->hrck', rpb.astype(F32), jnp.asarray(e_col), precision=hi)
    bias = jnp.einsum('pqjr,hrck->phqcjk', jnp.asarray(e_row), t, precision=hi) * LOG2E
    ok = row_ok[:, None, :, None, :, None] & col_ok[None, None, None, :, None, :]
    bias = jnp.where(jnp.asarray(ok), bias, MASK_VALUE)
    return bias.reshape(npat, NA_HEADS, NA_TQ, NA_SK)


def _na_attn_kernel(start_ref, pat_ref, q_ref, k_ref, v_ref, kc_ref, vc_ref, bias_ref, o_ref):
    del pat_ref
    i = pl.program_id(1)
    off = pl.multiple_of(start_ref[i], GRID_W)
    q = q_ref[...]
    s_loc = _dot_nt(q, k_ref[pl.ds(off, NA_SK), :]) + bias_ref[...]
    s_ctx = _dot_nt(q, kc_ref[...])
    m = jnp.maximum(jnp.max(s_loc, axis=-1, keepdims=True), jnp.max(s_ctx, axis=-1, keepdims=True))
    p_loc = jnp.exp2(s_loc - m)
    p_ctx = jnp.exp2(s_ctx - m)
    l = jnp.sum(p_loc, axis=-1, keepdims=True) + jnp.sum(p_ctx, axis=-1, keepdims=True)
    o = _dot(p_loc.astype(BF16), v_ref[pl.ds(off, NA_SK), :]) + _dot(p_ctx.astype(BF16), vc_ref[...])
    o_ref[...] = (o / l).astype(BF16)


def _na_attn(qkv, qkvc, bias, starts, pats):
    l = qkv.shape[0]
    lc = qkvc.shape[0]
    grid_spec = pltpu.PrefetchScalarGridSpec(
        num_scalar_prefetch=2,
        grid=(NA_HEADS, l // NA_TQ),
        in_specs=[
            pl.BlockSpec((NA_TQ, HEAD_DIM), lambda h, i, st, pt: (i, QN_COL + h)),
            pl.BlockSpec((l, HEAD_DIM), lambda h, i, st, pt: (0, KN_COL + h)),
            pl.BlockSpec((l, HEAD_DIM), lambda h, i, st, pt: (0, VN_COL + h)),
            pl.BlockSpec((lc, HEAD_DIM), lambda h, i, st, pt: (0, KN_COL + h)),
            pl.BlockSpec((lc, HEAD_DIM), lambda h, i, st, pt: (0, VN_COL + h)),
            pl.BlockSpec((None, None, NA_TQ, NA_SK), lambda h, i, st, pt: (pt[i], h, 0, 0)),
        ],
        out_specs=pl.BlockSpec((NA_TQ, HEAD_DIM), lambda h, i, st, pt: (i, h)),
    )
    return pl.pallas_call(
        _na_attn_kernel,
        grid_spec=grid_spec,
        out_shape=jax.ShapeDtypeStruct((l, NA_WIDTH), BF16),
        compiler_params=_params(("parallel", "arbitrary")),
        name="na_attn",
    )(starts, pats, qkv, qkv, qkv, qkvc, qkvc, bias)


def _softmax_pv(q, k, v):
    s = _dot_nt(q, k)
    m = jnp.max(s, axis=-1, keepdims=True)
    p = jnp.exp2(s - m)
    l = jnp.sum(p, axis=-1, keepdims=True)
    return _dot(p.astype(BF16), v) / l


def _ctx_attn_kernel(x_ref, lam_ref, g_ref, od_ref, on_ref, *, lambda_init):
    lam = _lambda_value(lam_ref[...], lambda_init)
    g = g_ref[...]

    def col(base, h, width=HEAD_DIM):
        lo = (base + h) * HEAD_DIM
        return x_ref[:, lo:lo + width]

    for h in range(DA_HEADS):
        v = col(VD_COL, 2 * h, 2 * HEAD_DIM)
        o1 = _softmax_pv(col(QD_COL, 2 * h), col(KD_COL, 2 * h), v)
        o2 = _softmax_pv(col(QD_COL, 2 * h + 1), col(KD_COL, 2 * h + 1), v)
        od_ref[:, h * 256:(h + 1) * 256] = _subln(o1 - lam * o2, g, lambda_init).astype(BF16)
    for h in range(NA_HEADS):
        o = _softmax_pv(col(QN_COL, h), col(KN_COL, h), col(VN_COL, h))
        on_ref[:, h * HEAD_DIM:(h + 1) * HEAD_DIM] = o.astype(BF16)


def _ctx_attn(qkvc, lam_vec, subln_g, lambda_init):
    lc = qkvc.shape[0]
    kern = functools.partial(_ctx_attn_kernel, lambda_init=lambda_init)
    return pl.pallas_call(
        kern,
        grid=(1,),
        in_specs=[
            pl.BlockSpec((lc, IN_DIM), lambda i: (0, 0)),
            pl.BlockSpec((4, HEAD_DIM), lambda i: (0, 0)),
            pl.BlockSpec((1, 2 * HEAD_DIM), lambda i: (0, 0)),
        ],
        out_specs=[pl.BlockSpec((lc, DA_WIDTH), lambda i: (0, 0)),
                   pl.BlockSpec((lc, NA_WIDTH), lambda i: (0, 0))],
        out_shape=[jax.ShapeDtypeStruct((lc, DA_WIDTH), BF16),
                   jax.ShapeDtypeStruct((lc, NA_WIDTH), BF16)],
        compiler_params=_params(("arbitrary",)),
        name="ctx_attn",
    )(qkvc, lam_vec, subln_g)


def _out_proj_kernel(od_ref, on_ref, w_ref, x_ref, gate_ref, g_ref, b_ref, o_ref):
    y = _dot(od_ref[...], w_ref[0:DA_WIDTH, :]) + _dot(on_ref[...], w_ref[DA_WIDTH:, :])
    z = ALPHA * x_ref[...] + gate_ref[...] * y
    o_ref[...] = _layernorm_rows(z, g_ref[...], b_ref[...])


def _out_proj_ln(od, on, w, x, gate, ln_g, ln_b, tm):
    m = x.shape[0]
    vec = pl.BlockSpec((1, D_MODEL), lambda i: (0, 0))
    return pl.pallas_call(
        _out_proj_kernel,
        grid=(m // tm,),
        in_specs=[
            pl.BlockSpec((tm, DA_WIDTH), lambda i: (i, 0)),
            pl.BlockSpec((tm, NA_WIDTH), lambda i: (i, 0)),
            pl.BlockSpec((DA_WIDTH + NA_WIDTH, D_MODEL), lambda i: (0, 0)),
            pl.BlockSpec((tm, D_MODEL), lambda i: (i, 0)),
            vec, vec, vec,
        ],
        out_specs=pl.BlockSpec((tm, D_MODEL), lambda i: (i, 0)),
        out_shape=jax.ShapeDtypeStruct((m, D_MODEL), F32),
        compiler_params=_params(("parallel",)),
        name="out_proj_ln",
    )(od, on, w, x, gate, ln_g, ln_b)


FFN_TF = 512
HALO = 8


def _ffn_kernel(x_ref, xp_ref, xn_ref, sc_ref, sh_ref, gate_ref, wg_ref, wu_ref, cw_ref, cb_ref, wd_ref,
                g_ref, b_ref, o_ref, h_ref, hh_ref, acc_ref):
    i = pl.program_id(0)
    c = pl.program_id(1)
    tm = x_ref.shape[0]

    @pl.when(c == 0)
    def _():
        sc = 1.0 + sc_ref[...]
        sh = sh_ref[...]
        h_ref[...] = (x_ref[...] * sc + sh).astype(BF16)
        hh_ref[0:HALO, :] = (xp_ref[...] * sc + sh).astype(BF16)
        hh_ref[HALO:2 * HALO, :] = (xn_ref[...] * sc + sh).astype(BF16)
        acc_ref[...] = jnp.zeros_like(acc_ref)

    h = h_ref[...]
    wg = wg_ref[...]
    g = _dot(h, wg)
    u = _dot(h, wu_ref[...])
    gh = _dot(hh_ref[...], wg)
    g_before = jnp.where(i == 0, 0.0, gh[HALO - 1:HALO, :])
    g_after = jnp.where(i == pl.num_programs(0) - 1, 0.0, gh[HALO:HALO + 1, :])
    row = lax.broadcasted_iota(jnp.int32, g.shape, 0)
    g_prev = jnp.where(row == 0, g_before, pltpu.roll(g, 1, 0))
    g_next = jnp.where(row == tm - 1, g_after, pltpu.roll(g, tm - 1, 0))
    gc = g_prev * cw_ref[0:1, :] + g * cw_ref[1:2, :] + g_next * cw_ref[2:3, :] + cb_ref[...]
    act = gc / (1.0 + jnp.exp(-gc)) * u
    acc_ref[...] += _dot(act.astype(BF16), wd_ref[...])

    @pl.when(c == pl.num_programs(1) - 1)
    def _():
        z = ALPHA * x_ref[...] + gate_ref[...] * acc_ref[...]
        o_ref[...] = _layernorm_rows(z, g_ref[...], b_ref[...])


def _ffn_ln(x, sc, sh, gate, w_up, conv_w, conv_b, w_down, ln_g, ln_b, tm):
    m = x.shape[0]
    nf = D_FF // FFN_TF
    tb = tm // HALO
    last_blk = m // HALO - 1
    vec = pl.BlockSpec((1, D_MODEL), lambda i, c: (0, 0))
    return pl.pallas_call(
        _ffn_kernel,
        grid=(m // tm, nf),
        in_specs=[
            pl.BlockSpec((tm, D_MODEL), lambda i, c: (i, 0)),
            pl.BlockSpec((HALO, D_MODEL), lambda i, c: (jnp.maximum(i * tb - 1, 0), 0)),
            pl.BlockSpec((HALO, D_MODEL), lambda i, c: (jnp.minimum((i + 1) * tb, last_blk), 0)),
            vec, vec, vec,
            pl.BlockSpec((D_MODEL, FFN_TF), lambda i, c: (0, c)),
            pl.BlockSpec((D_MODEL, FFN_TF), lambda i, c: (0, nf + c)),
            pl.BlockSpec((3, FFN_TF), lambda i, c: (0, c)),
            pl.BlockSpec((1, FFN_TF), lambda i, c: (0, c)),
            pl.BlockSpec((FFN_TF, D_MODEL), lambda i, c: (c, 0)),
            vec, vec,
        ],
        out_specs=pl.BlockSpec((tm, D_MODEL), lambda i, c: (i, 0)),
        out_shape=jax.ShapeDtypeStruct((m, D_MODEL), F32),
        scratch_shapes=[pltpu.VMEM((tm, D_MODEL), BF16),
                        pltpu.VMEM((2 * HALO, D_MODEL), BF16),
                        pltpu.VMEM((tm, D_MODEL), F32)],
        compiler_params=_params(("parallel", "arbitrary")),
        name="ffn_ln",
    )(x, x, x, sc, sh, gate, w_up, w_up, conv_w, conv_b, w_down, ln_g, ln_b)


def _rope_tables(l):
    t = jnp.arange(l, dtype=jnp.int32)
    pos_r = (t // GRID_W).astype(F32)
    pos_c = (t % GRID_W).astype(F32)
    half = HEAD_DIM // 2
    inv_freq = 1.0 / (ROPE_THETA ** (jnp.arange(0, half, 2, dtype=F32) / half))
    ar = pos_r[:, None] * inv_freq[None, :]
    ac = pos_c[:, None] * inv_freq[None, :]
    ang = jnp.concatenate([ar, ar, ac, ac], axis=-1)
    cos, sin = jnp.cos(ang), jnp.sin(ang)
    first = (np.arange(HEAD_DIM) % (HEAD_DIM // 2)) < (HEAD_DIM // 4)
    sin_a = jnp.where(jnp.asarray(first)[None, :], -sin, 0.0)
    sin_b = jnp.where(jnp.asarray(first)[None, :], 0.0, sin)
    return cos, sin_a, sin_b


def kernel(x, c, ctx, c_ctx, w_ada, b_ada, w_in, da_lambda, da_subln, na_rpb, w_o,
           ln1_g, ln1_b, w_up, conv_w, conv_b, w_down, ln2_g, ln2_b):
    assert x.shape[0] == 1 and ctx.shape[0] == 1
    l = x.shape[1]
    lc = ctx.shape[1]
    xs = x[0]
    xc = ctx[0]

    cc = jnp.zeros((8, D_MODEL), F32).at[0].set(c[0]).at[1].set(c_ctx)
    mods = _ada_mod(cc, w_ada, b_ada)

    cos, sin_a, sin_b = _rope_tables(l)
    ones_c = jnp.ones((lc, HEAD_DIM), F32)
    zeros_c = jnp.zeros((lc, HEAD_DIM), F32)
    starts, pats, geoms = _na_plan(l)
    starts = jnp.asarray(starts)
    pats = jnp.asarray(pats)

    for layer in range(DEPTH):
        last = layer == DEPTH - 1
        lambda_init = 0.8 - 0.6 * math.exp(-0.3 * layer)
        sh_a, sc_a, g_a, sh_m, sc_m, g_m = jnp.split(mods[layer, 0:1], N_MOD, axis=-1)
        shc_a, scc_a, gc_a, shc_m, scc_m, gc_m = jnp.split(mods[layer, 1:2], N_MOD, axis=-1)
        w_in_l = w_in[layer].astype(BF16)
        w_o_l = w_o[layer].astype(BF16)
        w_up_l = w_up[layer].astype(BF16)
        w_down_l = w_down[layer].astype(BF16)
        lam_vec = da_lambda[layer]
        subln_g = da_subln[layer].reshape(1, 2 * HEAD_DIM)
        ln1 = (ln1_g[layer].reshape(1, D_MODEL), ln1_b[layer].reshape(1, D_MODEL))
        ln2 = (ln2_g[layer].reshape(1, D_MODEL), ln2_b[layer].reshape(1, D_MODEL))
        cw = conv_w[layer]
        cb = conv_b[layer].reshape(1, D_FF)

        qkv = _in_proj(xs, sc_a, sh_a, w_in_l, cos, sin_a, sin_b, tm=512)
        qkvc = _in_proj(xc, scc_a, shc_a, w_in_l, ones_c, zeros_c, zeros_c, tm=lc)

        od = _diff_attn(qkv, qkvc, lam_vec, subln_g, lambda_init, tq=512, tk=512)
        bias = _na_bias_table(na_rpb[layer], geoms)
        on = _na_attn(qkv, qkvc, bias, starts, pats)

        xs = _out_proj_ln(od, on, w_o_l, xs, g_a, *ln1, tm=256)
        xs = _ffn_ln(xs, sc_m, sh_m, g_m, w_up_l, cw, cb, w_down_l, *ln2, tm=512)

        if not last:
            odc, onc = _ctx_attn(qkvc, lam_vec, subln_g, lambda_init)
            xc = _out_proj_ln(odc, onc, w_o_l, xc, gc_a, *ln1, tm=lc)
            xc = _ffn_ln(xc, scc_m, shc_m, gc_m, w_up_l, cw, cb, w_down_l, *ln2, tm=lc)

    return xs[None]
```

```python
import functools
import math

import numpy as np
import jax
import jax.numpy as jnp
from jax import lax
from jax.experimental import pallas as pl
from jax.experimental.pallas import tpu as pltpu

D_MODEL = 2048
DEPTH = 2
GRID_W = 64
HEAD_DIM = 128
DA_HEADS = 4
NA_HEADS = 8
DA_WIDTH = DA_HEADS * 2 * HEAD_DIM
NA_WIDTH = NA_HEADS * HEAD_DIM
IN_DIM = 3 * DA_WIDTH + 3 * NA_WIDTH
NA_KH = 8
NA_KW = 16
D_FF = 5632
ROPE_THETA = 10000.0
LN_EPS = 1e-5
N_MOD = 6
ALPHA = (2.0 * DEPTH) ** 0.25
LOG2E = math.log2(math.e)
Q_SCALE = (HEAD_DIM ** -0.5) * LOG2E
MASK_VALUE = -1e30

F32 = jnp.float32
BF16 = jnp.bfloat16

VMEM_LIMIT = 56 * 1024 * 1024

QD_COL, KD_COL, VD_COL = 0, DA_WIDTH // 128, 2 * DA_WIDTH // 128
QN_COL = 3 * DA_WIDTH // 128
KN_COL = QN_COL + NA_WIDTH // 128
VN_COL = KN_COL + NA_WIDTH // 128


def _params(semantics):
    return pltpu.CompilerParams(dimension_semantics=semantics, vmem_limit_bytes=VMEM_LIMIT)


def _dot(a, b):
    return jnp.dot(a, b, preferred_element_type=F32)


def _dot_nt(a, b):
    return lax.dot_general(a, b, (((1,), (1,)), ((), ())), preferred_element_type=F32)


def _layernorm_rows(z, g, b):
    mu = jnp.mean(z, axis=-1, keepdims=True)
    zc = z - mu
    var = jnp.mean(zc * zc, axis=-1, keepdims=True)
    return zc * lax.rsqrt(var + LN_EPS) * g + b


ADA_TN = 1024


def _ada_kernel(c_ref, w_ref, b_ref, o_ref):
    a = c_ref[...]
    s = a / (1.0 + jnp.exp(-a))
    o_ref[...] = jnp.dot(s, w_ref[...], preferred_element_type=F32,
                         precision=lax.Precision.HIGHEST) + b_ref[...]


def _ada_mod(cc, w_ada, b_ada):
    nmod = N_MOD * D_MODEL
    return pl.pallas_call(
        _ada_kernel,
        grid=(DEPTH, nmod // ADA_TN),
        in_specs=[
            pl.BlockSpec((8, D_MODEL), lambda l, j: (0, 0)),
            pl.BlockSpec((None, D_MODEL, ADA_TN), lambda l, j: (l, 0, j)),
            pl.BlockSpec((None, 1, ADA_TN), lambda l, j: (l, 0, j)),
        ],
        out_specs=pl.BlockSpec((None, 8, ADA_TN), lambda l, j: (l, 0, j)),
        out_shape=jax.ShapeDtypeStruct((DEPTH, 8, nmod), F32),
        compiler_params=_params(("parallel", "parallel")),
        name="ada_mod",
    )(cc, w_ada, b_ada.reshape(DEPTH, 1, nmod))


PROJ_TN = 1024


def _in_proj_kernel(x_ref, sc_ref, sh_ref, w_ref, cos_ref, sa_ref, sb_ref, o_ref, h_ref):
    j = pl.program_id(1)

    @pl.when(j == 0)
    def _():
        h_ref[...] = (x_ref[...] * (1.0 + sc_ref[...]) + sh_ref[...]).astype(BF16)

    acc = _dot(h_ref[...], w_ref[...])

    @pl.when(j <= 1)
    def _():
        scale = jnp.where(j == 0, Q_SCALE, 1.0).astype(F32)
        cos = cos_ref[...] * scale
        sa = sa_ref[...] * scale
        sb = sb_ref[...] * scale
        for k in range(PROJ_TN // HEAD_DIM):
            a = acc[:, k * HEAD_DIM:(k + 1) * HEAD_DIM]
            r = a * cos + pltpu.roll(a, 96, 1) * sa + pltpu.roll(a, 32, 1) * sb
            o_ref[:, k * HEAD_DIM:(k + 1) * HEAD_DIM] = r.astype(BF16)

    @pl.when(j == 3)
    def _():
        o_ref[...] = (acc * Q_SCALE).astype(BF16)

    @pl.when((j == 2) | (j >= 4))
    def _():
        o_ref[...] = acc.astype(BF16)


def _in_proj(x, sc, sh, w, cos, sa, sb, tm):
    m = x.shape[0]
    return pl.pallas_call(
        _in_proj_kernel,
        grid=(m // tm, IN_DIM // PROJ_TN),
        in_specs=[
            pl.BlockSpec((tm, D_MODEL), lambda i, j: (i, 0)),
            pl.BlockSpec((1, D_MODEL), lambda i, j: (0, 0)),
            pl.BlockSpec((1, D_MODEL), lambda i, j: (0, 0)),
            pl.BlockSpec((D_MODEL, PROJ_TN), lambda i, j: (0, j)),
            pl.BlockSpec((tm, HEAD_DIM), lambda i, j: (i, 0)),
            pl.BlockSpec((tm, HEAD_DIM), lambda i, j: (i, 0)),
            pl.BlockSpec((tm, HEAD_DIM), lambda i, j: (i, 0)),
        ],
        out_specs=pl.BlockSpec((tm, PROJ_TN), lambda i, j: (i, j)),
        out_shape=jax.ShapeDtypeStruct((m, IN_DIM), BF16),
        scratch_shapes=[pltpu.VMEM((tm, D_MODEL), BF16)],
        compiler_params=_params(("parallel", "arbitrary")),
        name="in_proj",
    )(x, sc, sh, w, cos, sa, sb)


def _lambda_value(lv, lambda_init):
    a = jnp.sum(lv[0:1, :] * lv[1:2, :], axis=-1, keepdims=True)
    b = jnp.sum(lv[2:3, :] * lv[3:4, :], axis=-1, keepdims=True)
    return jnp.exp(a) - jnp.exp(b) + lambda_init


def _subln(o, g, lambda_init):
    ms = jnp.mean(o * o, axis=-1, keepdims=True)
    return o * lax.rsqrt(ms + LN_EPS) * g * (1.0 - lambda_init)


def _diff_attn_kernel(q_ref, k_ref, v_ref, kc_ref, vc_ref, lam_ref, g_ref, o_ref,
                      acc1_ref, acc2_ref, sa1_ref, sa2_ref, sb1_ref, sb2_ref, *, tk, lambda_init):
    nk = k_ref.shape[0] // tk
    assert nk % 2 == 0 and nk >= 2
    qs = (q_ref[:, 0:HEAD_DIM], q_ref[:, HEAD_DIM:2 * HEAD_DIM])
    accs = (acc1_ref, acc2_ref)
    buf_a = (sa1_ref, sa2_ref)
    buf_b = (sb1_ref, sb2_ref)

    def scores_into(i, dst):
        off = pl.multiple_of(i * tk, tk)
        mx = []
        for c in range(2):
            s = _dot_nt(qs[c], k_ref[pl.ds(off, tk), c * HEAD_DIM:(c + 1) * HEAD_DIM])
            dst[c][...] = s
            mx.append(jnp.max(s, axis=-1, keepdims=True))
        return tuple(mx)

    def softmax_pv(i, src, mx, carry):
        off = pl.multiple_of(i * tk, tk)
        vb = v_ref[pl.ds(off, tk), :]
        out = []
        for c in range(2):
            m, l = carry[2 * c], carry[2 * c + 1]
            s = src[c][...]
            mn = jnp.maximum(m, mx[c])
            a = jnp.exp2(m - mn)
            p = jnp.exp2(s - mn)
            l = a * l + jnp.sum(p, axis=-1, keepdims=True)
            accs[c][...] = a * accs[c][...] + _dot(p.astype(BF16), vb)
            out += [mn, l]
        return tuple(out)

    mx_a = scores_into(0, buf_a)

    vc = vc_ref[...]
    carry = []
    for c in range(2):
        s = _dot_nt(qs[c], kc_ref[:, c * HEAD_DIM:(c + 1) * HEAD_DIM])
        m = jnp.max(s, axis=-1, keepdims=True)
        p = jnp.exp2(s - m)
        l = jnp.sum(p, axis=-1, keepdims=True)
        accs[c][...] = _dot(p.astype(BF16), vc)
        carry += [m, l]

    def body(j, state):
        mx_a, carry = state
        mx_b = scores_into(2 * j + 1, buf_b)
        carry = softmax_pv(2 * j, buf_a, mx_a, carry)
        mx_a = scores_into(2 * j + 2, buf_a)
        return mx_a, softmax_pv(2 * j + 1, buf_b, mx_b, carry)

    mx_a, carry = lax.fori_loop(0, nk // 2 - 1, body, (mx_a, tuple(carry)))
    mx_b = scores_into(nk - 1, buf_b)
    carry = softmax_pv(nk - 2, buf_a, mx_a, carry)
    m1, l1, m2, l2 = softmax_pv(nk - 1, buf_b, mx_b, carry)

    lam = _lambda_value(lam_ref[...], lambda_init)
    o = acc1_ref[...] / l1 - lam * (acc2_ref[...] / l2)
    o_ref[...] = _subln(o, g_ref[...], lambda_init).astype(BF16)


def _diff_attn(qkv, qkvc, lam_vec, subln_g, lambda_init, tq, tk):
    l = qkv.shape[0]
    lc = qkvc.shape[0]
    kern = functools.partial(_diff_attn_kernel, tk=tk, lambda_init=lambda_init)
    return pl.pallas_call(
        kern,
        grid=(DA_HEADS, l // tq),
        in_specs=[
            pl.BlockSpec((tq, 256), lambda h, i: (i, QD_COL // 2 + h)),
            pl.BlockSpec((l, 256), lambda h, i: (0, KD_COL // 2 + h)),
            pl.BlockSpec((l, 256), lambda h, i: (0, VD_COL // 2 + h)),
            pl.BlockSpec((lc, 256), lambda h, i: (0, KD_COL // 2 + h)),
            pl.BlockSpec((lc, 256), lambda h, i: (0, VD_COL // 2 + h)),
            pl.BlockSpec((4, HEAD_DIM), lambda h, i: (0, 0)),
            pl.BlockSpec((1, 2 * HEAD_DIM), lambda h, i: (0, 0)),
        ],
        out_specs=pl.BlockSpec((tq, 256), lambda h, i: (i, h)),
        out_shape=jax.ShapeDtypeStruct((l, DA_WIDTH), BF16),
        scratch_shapes=[pltpu.VMEM((tq, 256), F32), pltpu.VMEM((tq, 256), F32)]
        + [pltpu.VMEM((tq, tk), F32)] * 4,
        compiler_params=_params(("parallel", "arbitrary")),
        name="diff_attn",
    )(qkv, qkv, qkv, qkvc, qkvc, lam_vec, subln_g)


NA_ROWS = 4
NA_TQ = NA_ROWS * GRID_W
NA_SLAB = NA_ROWS + NA_KH - 1
NA_SK = NA_SLAB * GRID_W


def _na_plan(l):
    rows = l // GRID_W
    kh = min(NA_KH, rows)
    assert kh == NA_KH and rows >= NA_SLAB and rows % NA_ROWS == 0
    nblk = rows // NA_ROWS
    starts, pats, geoms = [], [], []
    for b in range(nblk):
        r0 = b * NA_ROWS
        kstart = int(np.clip(r0 - kh // 2, 0, rows - NA_SLAB))
        rs = [int(np.clip(r0 + qr - kh // 2, 0, rows - kh)) for qr in range(NA_ROWS)]
        geom = (r0 - kstart, tuple(x - kstart for x in rs))
        assert all(0 <= x and x + kh <= NA_SLAB for x in geom[1])
        if geom not in geoms:
            geoms.append(geom)
        starts.append(kstart * GRID_W)
        pats.append(geoms.index(geom))
    return np.asarray(starts, np.int32), np.asarray(pats, np.int32), geoms


def _na_bias_table(rpb, geoms):
    c = np.arange(GRID_W)
    dc = c[None, :] - c[:, None] + (NA_KW - 1)
    cs = np.clip(c - NA_KW // 2, 0, GRID_W - NA_KW)
    col_ok = (c[None, :] >= cs[:, None]) & (c[None, :] < cs[:, None] + NA_KW)
    e_col = np.zeros((2 * NA_KW - 1, GRID_W, GRID_W), np.float32)
    for ci in range(GRID_W):
        for ki in range(GRID_W):
            if col_ok[ci, ki]:
                e_col[dc[ci, ki], ci, ki] = 1.0
    npat = len(geoms)
    e_row = np.zeros((npat, NA_ROWS, NA_SLAB, 2 * NA_KH - 1), np.float32)
    row_ok = np.zeros((npat, NA_ROWS, NA_SLAB), bool)
    for p, (dq, rs) in enumerate(geoms):
        for qr in range(NA_ROWS):
            for j in range(NA_SLAB):
                if rs[qr] <= j < rs[qr] + NA_KH:
                    e_row[p, qr, j, j - (dq + qr) + (NA_KH - 1)] = 1.0
                    row_ok[p, qr, j] = True
    hi = lax.Precision.HIGHEST
    t = jnp.einsum('hrd,dck->hrck', rpb.astype(F32), jnp.asarray(e_col), precision=hi)
    bias = jnp.einsum('pqjr,hrck->phqcjk', jnp.asarray(e_row), t, precision=hi) * LOG2E
    ok = row_ok[:, None, :, None, :, None] & col_ok[None, None, None, :, None, :]
    bias = jnp.where(jnp.asarray(ok), bias, MASK_VALUE)
    return bias.reshape(npat, NA_HEADS, NA_TQ, NA_SK)


def _na_attn_kernel(start_ref, pat_ref, q_ref, k_ref, v_ref, kc_ref, vc_ref, bias_ref, o_ref):
    del pat_ref
    i = pl.program_id(1)
    off = pl.multiple_of(start_ref[i], GRID_W)
    q = q_ref[...]
    s_loc = _dot_nt(q, k_ref[pl.ds(off, NA_SK), :]) + bias_ref[...]
    s_ctx = _dot_nt(q, kc_ref[...])
    m = jnp.maximum(jnp.max(s_loc, axis=-1, keepdims=True), jnp.max(s_ctx, axis=-1, keepdims=True))
    p_loc = jnp.exp2(s_loc - m)
    p_ctx = jnp.exp2(s_ctx - m)
    l = jnp.sum(p_loc, axis=-1, keepdims=True) + jnp.sum(p_ctx, axis=-1, keepdims=True)
    o = _dot(p_loc.astype(BF16), v_ref[pl.ds(off, NA_SK), :]) + _dot(p_ctx.astype(BF16), vc_ref[...])
    o_ref[...] = (o / l).astype(BF16)


def _na_attn(qkv, qkvc, bias, starts, pats):
    l = qkv.shape[0]
    lc = qkvc.shape[0]
    grid_spec = pltpu.PrefetchScalarGridSpec(
        num_scalar_prefetch=2,
        grid=(NA_HEADS, l // NA_TQ),
        in_specs=[
            pl.BlockSpec((NA_TQ, HEAD_DIM), lambda h, i, st, pt: (i, QN_COL + h)),
            pl.BlockSpec((l, HEAD_DIM), lambda h, i, st, pt: (0, KN_COL + h)),
            pl.BlockSpec((l, HEAD_DIM), lambda h, i, st, pt: (0, VN_COL + h)),
            pl.BlockSpec((lc, HEAD_DIM), lambda h, i, st, pt: (0, KN_COL + h)),
            pl.BlockSpec((lc, HEAD_DIM), lambda h, i, st, pt: (0, VN_COL + h)),
            pl.BlockSpec((None, None, NA_TQ, NA_SK), lambda h, i, st, pt: (pt[i], h, 0, 0)),
        ],
        out_specs=pl.BlockSpec((NA_TQ, HEAD_DIM), lambda h, i, st, pt: (i, h)),
    )
    return pl.pallas_call(
        _na_attn_kernel,
        grid_spec=grid_spec,
        out_shape=jax.ShapeDtypeStruct((l, NA_WIDTH), BF16),
        compiler_params=_params(("parallel", "arbitrary")),
        name="na_attn",
    )(starts, pats, qkv, qkv, qkv, qkvc, qkvc, bias)


def _softmax_pv(q, k, v):
    s = _dot_nt(q, k)
    m = jnp.max(s, axis=-1, keepdims=True)
    p = jnp.exp2(s - m)
    l = jnp.sum(p, axis=-1, keepdims=True)
    return _dot(p.astype(BF16), v) / l


def _ctx_attn_kernel(x_ref, lam_ref, g_ref, od_ref, on_ref, *, lambda_init):
    lam = _lambda_value(lam_ref[...], lambda_init)
    g = g_ref[...]

    def col(base, h, width=HEAD_DIM):
        lo = (base + h) * HEAD_DIM
        return x_ref[:, lo:lo + width]

    for h in range(DA_HEADS):
        v = col(VD_COL, 2 * h, 2 * HEAD_DIM)
        o1 = _softmax_pv(col(QD_COL, 2 * h), col(KD_COL, 2 * h), v)
        o2 = _softmax_pv(col(QD_COL, 2 * h + 1), col(KD_COL, 2 * h + 1), v)
        od_ref[:, h * 256:(h + 1) * 256] = _subln(o1 - lam * o2, g, lambda_init).astype(BF16)
    for h in range(NA_HEADS):
        o = _softmax_pv(col(QN_COL, h), col(KN_COL, h), col(VN_COL, h))
        on_ref[:, h * HEAD_DIM:(h + 1) * HEAD_DIM] = o.astype(BF16)


def _ctx_attn(qkvc, lam_vec, subln_g, lambda_init):
    lc = qkvc.shape[0]
    kern = functools.partial(_ctx_attn_kernel, lambda_init=lambda_init)
    return pl.pallas_call(
        kern,
        grid=(1,),
        in_specs=[
            pl.BlockSpec((lc, IN_DIM), lambda i: (0, 0)),
            pl.BlockSpec((4, HEAD_DIM), lambda i: (0, 0)),
            pl.BlockSpec((1, 2 * HEAD_DIM), lambda i: (0, 0)),
        ],
        out_specs=[pl.BlockSpec((lc, DA_WIDTH), lambda i: (0, 0)),
                   pl.BlockSpec((lc, NA_WIDTH), lambda i: (0, 0))],
        out_shape=[jax.ShapeDtypeStruct((lc, DA_WIDTH), BF16),
                   jax.ShapeDtypeStruct((lc, NA_WIDTH), BF16)],
        compiler_params=_params(("arbitrary",)),
        name="ctx_attn",
    )(qkvc, lam_vec, subln_g)


def _out_proj_kernel(od_ref, on_ref, w_ref, x_ref, gate_ref, g_ref, b_ref, o_ref):
    y = _dot(od_ref[...], w_ref[0:DA_WIDTH, :]) + _dot(on_ref[...], w_ref[DA_WIDTH:, :])
    z = ALPHA * x_ref[...] + gate_ref[...] * y
    o_ref[...] = _layernorm_rows(z, g_ref[...], b_ref[...])


def _out_proj_ln(od, on, w, x, gate, ln_g, ln_b, tm):
    m = x.shape[0]
    vec = pl.BlockSpec((1, D_MODEL), lambda i: (0, 0))
    return pl.pallas_call(
        _out_proj_kernel,
        grid=(m // tm,),
        in_specs=[
            pl.BlockSpec((tm, DA_WIDTH), lambda i: (i, 0)),
            pl.BlockSpec((tm, NA_WIDTH), lambda i: (i, 0)),
            pl.BlockSpec((DA_WIDTH + NA_WIDTH, D_MODEL), lambda i: (0, 0)),
            pl.BlockSpec((tm, D_MODEL), lambda i: (i, 0)),
            vec, vec, vec,
        ],
        out_specs=pl.BlockSpec((tm, D_MODEL), lambda i: (i, 0)),
        out_shape=jax.ShapeDtypeStruct((m, D_MODEL), F32),
        compiler_params=_params(("parallel",)),
        name="out_proj_ln",
    )(od, on, w, x, gate, ln_g, ln_b)


FFN_TF = 512
HALO = 8


def _ffn_kernel(x_ref, xp_ref, xn_ref, sc_ref, sh_ref, gate_ref, wg_ref, wu_ref, cw_ref, cb_ref, wd_ref,
                g_ref, b_ref, o_ref, h_ref, hh_ref, acc_ref):
    i = pl.program_id(0)
    c = pl.program_id(1)
    tm = x_ref.shape[0]

    @pl.when(c == 0)
    def _():
        sc = 1.0 + sc_ref[...]
        sh = sh_ref[...]
        h_ref[...] = (x_ref[...] * sc + sh).astype(BF16)
        hh_ref[0:HALO, :] = (xp_ref[...] * sc + sh).astype(BF16)
        hh_ref[HALO:2 * HALO, :] = (xn_ref[...] * sc + sh).astype(BF16)
        acc_ref[...] = jnp.zeros_like(acc_ref)

    h = h_ref[...]
    wg = wg_ref[...]
    g = _dot(h, wg)
    u = _dot(h, wu_ref[...])
    gh = _dot(hh_ref[...], wg)
    g_before = jnp.where(i == 0, 0.0, gh[HALO - 1:HALO, :])
    g_after = jnp.where(i == pl.num_programs(0) - 1, 0.0, gh[HALO:HALO + 1, :])
    row = lax.broadcasted_iota(jnp.int32, g.shape, 0)
    g_prev = jnp.where(row == 0, g_before, pltpu.roll(g, 1, 0))
    g_next = jnp.where(row == tm - 1, g_after, pltpu.roll(g, tm - 1, 0))
    gc = g_prev * cw_ref[0:1, :] + g * cw_ref[1:2, :] + g_next * cw_ref[2:3, :] + cb_ref[...]
    act = gc / (1.0 + jnp.exp(-gc)) * u
    acc_ref[...] += _dot(act.astype(BF16), wd_ref[...])

    @pl.when(c == pl.num_programs(1) - 1)
    def _():
        z = ALPHA * x_ref[...] + gate_ref[...] * acc_ref[...]
        o_ref[...] = _layernorm_rows(z, g_ref[...], b_ref[...])


def _ffn_ln(x, sc, sh, gate, w_up, conv_w, conv_b, w_down, ln_g, ln_b, tm):
    m = x.shape[0]
    nf = D_FF // FFN_TF
    tb = tm // HALO
    last_blk = m // HALO - 1
    vec = pl.BlockSpec((1, D_MODEL), lambda i, c: (0, 0))
    return pl.pallas_call(
        _ffn_kernel,
        grid=(m // tm, nf),
        in_specs=[
            pl.BlockSpec((tm, D_MODEL), lambda i, c: (i, 0)),
            pl.BlockSpec((HALO, D_MODEL), lambda i, c: (jnp.maximum(i * tb - 1, 0), 0)),
            pl.BlockSpec((HALO, D_MODEL), lambda i, c: (jnp.minimum((i + 1) * tb, last_blk), 0)),
            vec, vec, vec,
            pl.BlockSpec((D_MODEL, FFN_TF), lambda i, c: (0, c)),
            pl.BlockSpec((D_MODEL, FFN_TF), lambda i, c: (0, nf + c)),
            pl.BlockSpec((3, FFN_TF), lambda i, c: (0, c)),
            pl.BlockSpec((1, FFN_TF), lambda i, c: (0, c)),
            pl.BlockSpec((FFN_TF, D_MODEL), lambda i, c: (c, 0)),
            vec, vec,
        ],
        out_specs=pl.BlockSpec((tm, D_MODEL), lambda i, c: (i, 0)),
        out_shape=jax.ShapeDtypeStruct((m, D_MODEL), F32),
        scratch_shapes=[pltpu.VMEM((tm, D_MODEL), BF16),
                        pltpu.VMEM((2 * HALO, D_MODEL), BF16),
                        pltpu.VMEM((tm, D_MODEL), F32)],
        compiler_params=_params(("parallel", "arbitrary")),
        name="ffn_ln",
    )(x, x, x, sc, sh, gate, w_up, w_up, conv_w, conv_b, w_down, ln_g, ln_b)


def _rope_tables(l):
    t = jnp.arange(l, dtype=jnp.int32)
    pos_r = (t // GRID_W).astype(F32)
    pos_c = (t % GRID_W).astype(F32)
    half = HEAD_DIM // 2
    inv_freq = 1.0 / (ROPE_THETA ** (jnp.arange(0, half, 2, dtype=F32) / half))
    ar = pos_r[:, None] * inv_freq[None, :]
    ac = pos_c[:, None] * inv_freq[None, :]
    ang = jnp.concatenate([ar, ar, ac, ac], axis=-1)
    cos, sin = jnp.cos(ang), jnp.sin(ang)
    first = (np.arange(HEAD_DIM) % (HEAD_DIM // 2)) < (HEAD_DIM // 4)
    sin_a = jnp.where(jnp.asarray(first)[None, :], -sin, 0.0)
    sin_b = jnp.where(jnp.asarray(first)[None, :], 0.0, sin)
    return cos, sin_a, sin_b


def kernel(x, c, ctx, c_ctx, w_ada, b_ada, w_in, da_lambda, da_subln, na_rpb, w_o,
           ln1_g, ln1_b, w_up, conv_w, conv_b, w_down, ln2_g, ln2_b):
    assert x.shape[0] == 1 and ctx.shape[0] == 1
    l = x.shape[1]
    lc = ctx.shape[1]
    xs = x[0]
    xc = ctx[0]

    cc = jnp.zeros((8, D_MODEL), F32).at[0].set(c[0]).at[1].set(c_ctx)
    mods = _ada_mod(cc, w_ada, b_ada)

    cos, sin_a, sin_b = _rope_tables(l)
    ones_c = jnp.ones((lc, HEAD_DIM), F32)
    zeros_c = jnp.zeros((lc, HEAD_DIM), F32)
    starts, pats, geoms = _na_plan(l)
    starts = jnp.asarray(starts)
    pats = jnp.asarray(pats)

    for layer in range(DEPTH):
        last = layer == DEPTH - 1
        lambda_init = 0.8 - 0.6 * math.exp(-0.3 * layer)
        sh_a, sc_a, g_a, sh_m, sc_m, g_m = jnp.split(mods[layer, 0:1], N_MOD, axis=-1)
        shc_a, scc_a, gc_a, shc_m, scc_m, gc_m = jnp.split(mods[layer, 1:2], N_MOD, axis=-1)
        w_in_l = w_in[layer].astype(BF16)
        w_o_l = w_o[layer].astype(BF16)
        w_up_l = w_up[layer].astype(BF16)
        w_down_l = w_down[layer].astype(BF16)
        lam_vec = da_lambda[layer]
        subln_g = da_subln[layer].reshape(1, 2 * HEAD_DIM)
        ln1 = (ln1_g[layer].reshape(1, D_MODEL), ln1_b[layer].reshape(1, D_MODEL))
        ln2 = (ln2_g[layer].reshape(1, D_MODEL), ln2_b[layer].reshape(1, D_MODEL))
        cw = conv_w[layer]
        cb = conv_b[layer].reshape(1, D_FF)

        qkv = _in_proj(xs, sc_a, sh_a, w_in_l, cos, sin_a, sin_b, tm=512)
        qkvc = _in_proj(xc, scc_a, shc_a, w_in_l, ones_c, zeros_c, zeros_c, tm=lc)

        od = _diff_attn(qkv, qkvc, lam_vec, subln_g, lambda_init, tq=512, tk=512)
        bias = _na_bias_table(na_rpb[layer], geoms)
        on = _na_attn(qkv, qkvc, bias, starts, pats)

        xs = _out_proj_ln(od, on, w_o_l, xs, g_a, *ln1, tm=256)
        xs = _ffn_ln(xs, sc_m, sh_m, g_m, w_up_l, cw, cb, w_down_l, *ln2, tm=512)

        if not last:
            odc, onc = _ctx_attn(qkvc, lam_vec, subln_g, lambda_init)
            xc = _out_proj_ln(odc, onc, w_o_l, xc, gc_a, *ln1, tm=lc)
            xc = _ffn_ln(xc, scc_m, shc_m, gc_m, w_up_l, cw, cb, w_down_l, *ln2, tm=lc)

    return xs[None]
```

```python
import functools
import math

import numpy as np
import jax
import jax.numpy as jnp
from jax import lax
from jax.experimental import pallas as pl
from jax.experimental.pallas import tpu as pltpu

D_MODEL = 2048
DEPTH = 2
GRID_W = 64
HEAD_DIM = 128
DA_HEADS = 4
NA_HEADS = 8
DA_WIDTH = DA_HEADS * 2 * HEAD_DIM
NA_WIDTH = NA_HEADS * HEAD_DIM
IN_DIM = 3 * DA_WIDTH + 3 * NA_WIDTH
NA_KH = 8
NA_KW = 16
D_FF = 5632
ROPE_THETA = 10000.0
LN_EPS = 1e-5
N_MOD = 6
ALPHA = (2.0 * DEPTH) ** 0.25
LOG2E = math.log2(math.e)
Q_SCALE = (HEAD_DIM ** -0.5) * LOG2E
MASK_VALUE = -1e30

F32 = jnp.float32
BF16 = jnp.bfloat16

VMEM_LIMIT = 56 * 1024 * 1024

QD_COL, KD_COL, VD_COL = 0, DA_WIDTH // 128, 2 * DA_WIDTH // 128
QN_COL = 3 * DA_WIDTH // 128
KN_COL = QN_COL + NA_WIDTH // 128
VN_COL = KN_COL + NA_WIDTH // 128


def _params(semantics):
    return pltpu.CompilerParams(dimension_semantics=semantics, vmem_limit_bytes=VMEM_LIMIT)


def _dot(a, b):
    return jnp.dot(a, b, preferred_element_type=F32)


def _dot_nt(a, b):
    return lax.dot_general(a, b, (((1,), (1,)), ((), ())), preferred_element_type=F32)


def _layernorm_rows(z, g, b):
    mu = jnp.mean(z, axis=-1, keepdims=True)
    zc = z - mu
    var = jnp.mean(zc * zc, axis=-1, keepdims=True)
    return zc * lax.rsqrt(var + LN_EPS) * g + b


ADA_TN = 1024


def _ada_kernel(c_ref, w_ref, b_ref, o_ref):
    a = c_ref[...]
    s = a / (1.0 + jnp.exp(-a))
    o_ref[...] = jnp.dot(s, w_ref[...], preferred_element_type=F32,
                         precision=lax.Precision.HIGHEST) + b_ref[...]


def _ada_mod(cc, w_ada, b_ada):
    nmod = N_MOD * D_MODEL
    return pl.pallas_call(
        _ada_kernel,
        grid=(DEPTH, nmod // ADA_TN),
        in_specs=[
            pl.BlockSpec((8, D_MODEL), lambda l, j: (0, 0)),
            pl.BlockSpec((None, D_MODEL, ADA_TN), lambda l, j: (l, 0, j)),
            pl.BlockSpec((None, 1, ADA_TN), lambda l, j: (l, 0, j)),
        ],
        out_specs=pl.BlockSpec((None, 8, ADA_TN), lambda l, j: (l, 0, j)),
        out_shape=jax.ShapeDtypeStruct((DEPTH, 8, nmod), F32),
        compiler_params=_params(("parallel", "parallel")),
        name="ada_mod",
    )(cc, w_ada, b_ada.reshape(DEPTH, 1, nmod))


PROJ_TN = 1024


def _in_proj_kernel(x_ref, sc_ref, sh_ref, w_ref, cos_ref, sa_ref, sb_ref, o_ref, h_ref):
    h_ref[...] = (x_ref[...] * (1.0 + sc_ref[...]) + sh_ref[...]).astype(BF16)
    for j in range(IN_DIM // PROJ_TN):
        acc = _dot(h_ref[...], w_ref[:, j * PROJ_TN:(j + 1) * PROJ_TN])
        if j <= 1:
            scale = Q_SCALE if j == 0 else 1.0
            cos = cos_ref[...] * scale
            sa = sa_ref[...] * scale
            sb = sb_ref[...] * scale
            for k in range(PROJ_TN // HEAD_DIM):
                a = acc[:, k * HEAD_DIM:(k + 1) * HEAD_DIM]
                r = a * cos + pltpu.roll(a, 96, 1) * sa + pltpu.roll(a, 32, 1) * sb
                o_ref[:, j * PROJ_TN + k * HEAD_DIM:j * PROJ_TN + (k + 1) * HEAD_DIM] = r.astype(BF16)
        elif j == 3:
            o_ref[:, j * PROJ_TN:(j + 1) * PROJ_TN] = (acc * Q_SCALE).astype(BF16)
        else:
            o_ref[:, j * PROJ_TN:(j + 1) * PROJ_TN] = acc.astype(BF16)


def _in_proj(x, sc, sh, w, cos, sa, sb, tm):
    m = x.shape[0]
    return pl.pallas_call(
        _in_proj_kernel,
        grid=(m // tm,),
        in_specs=[
            pl.BlockSpec((tm, D_MODEL), lambda i: (i, 0)),
            pl.BlockSpec((1, D_MODEL), lambda i: (0, 0)),
            pl.BlockSpec((1, D_MODEL), lambda i: (0, 0)),
            pl.BlockSpec((D_MODEL, IN_DIM), lambda i: (0, 0), pipeline_mode=pl.Buffered(1)),
            pl.BlockSpec((tm, HEAD_DIM), lambda i: (i, 0)),
            pl.BlockSpec((tm, HEAD_DIM), lambda i: (i, 0)),
            pl.BlockSpec((tm, HEAD_DIM), lambda i: (i, 0)),
        ],
        out_specs=pl.BlockSpec((tm, IN_DIM), lambda i: (i, 0)),
        out_shape=jax.ShapeDtypeStruct((m, IN_DIM), BF16),
        scratch_shapes=[pltpu.VMEM((tm, D_MODEL), BF16)],
        compiler_params=_params(("parallel",)),
        name="in_proj",
    )(x, sc, sh, w, cos, sa, sb)


def _lambda_value(lv, lambda_init):
    a = jnp.sum(lv[0:1, :] * lv[1:2, :], axis=-1, keepdims=True)
    b = jnp.sum(lv[2:3, :] * lv[3:4, :], axis=-1, keepdims=True)
    return jnp.exp(a) - jnp.exp(b) + lambda_init


def _subln(o, g, lambda_init):
    ms = jnp.mean(o * o, axis=-1, keepdims=True)
    return o * lax.rsqrt(ms + LN_EPS) * g * (1.0 - lambda_init)


def _diff_attn_kernel(q_ref, k_ref, v_ref, kc_ref, vc_ref, lam_ref, g_ref, o_ref,
                      vt_ref, acc1_ref, acc2_ref, sa1_ref, sa2_ref, sb1_ref, sb2_ref, *, tk, lambda_init):
    nk = k_ref.shape[0] // tk
    assert nk % 2 == 0 and nk >= 2
    accs = (acc1_ref, acc2_ref)
    buf_a = (sa1_ref, sa2_ref)
    buf_b = (sb1_ref, sb2_ref)

    @pl.when(pl.program_id(1) == 0)
    def _():
        def fill(i, carry):
            off = pl.multiple_of(i * tk, tk)
            vt_ref[i] = v_ref[pl.ds(off, tk), :].astype(F32).T.astype(BF16)
            return carry
        lax.fori_loop(0, nk, fill, 0)

    q_t = q_ref[...].astype(F32).T.astype(BF16)
    qs = (q_t[0:HEAD_DIM, :], q_t[HEAD_DIM:2 * HEAD_DIM, :])

    def scores_into(i, dst):
        off = pl.multiple_of(i * tk, tk)
        mx = []
        for c in range(2):
            s = _dot(k_ref[pl.ds(off, tk), c * HEAD_DIM:(c + 1) * HEAD_DIM], qs[c])
            dst[c][...] = s
            mx.append(jnp.max(s, axis=0, keepdims=True))
        return tuple(mx)

    def softmax_pv(i, src, mx, carry):
        vt = vt_ref[i]
        out = []
        for c in range(2):
            m, l = carry[2 * c], carry[2 * c + 1]
            mn = jnp.maximum(m, mx[c])
            a = jnp.exp2(m - mn)
            p = jnp.exp2(src[c][...] - mn)
            l = a * l + jnp.sum(p, axis=0, keepdims=True)
            accs[c][...] = a * accs[c][...] + _dot(vt, p.astype(BF16))
            out += [mn, l]
        return tuple(out)

    mx_a = scores_into(0, buf_a)

    vc_t = vc_ref[...].astype(F32).T.astype(BF16)
    carry = []
    for c in range(2):
        s = _dot(kc_ref[:, c * HEAD_DIM:(c + 1) * HEAD_DIM], qs[c])
        m = jnp.max(s, axis=0, keepdims=True)
        p = jnp.exp2(s - m)
        l = jnp.sum(p, axis=0, keepdims=True)
        accs[c][...] = _dot(vc_t, p.astype(BF16))
        carry += [m, l]

    def body(j, state):
        mx_a, carry = state
        mx_b = scores_into(2 * j + 1, buf_b)
        carry = softmax_pv(2 * j, buf_a, mx_a, carry)
        mx_a = scores_into(2 * j + 2, buf_a)
        return mx_a, softmax_pv(2 * j + 1, buf_b, mx_b, carry)

    mx_a, carry = lax.fori_loop(0, nk // 2 - 1, body, (mx_a, tuple(carry)))
    mx_b = scores_into(nk - 1, buf_b)
    carry = softmax_pv(nk - 2, buf_a, mx_a, carry)
    m1, l1, m2, l2 = softmax_pv(nk - 1, buf_b, mx_b, carry)

    lam = _lambda_value(lam_ref[...], lambda_init)
    o_t = acc1_ref[...] / l1 - lam * (acc2_ref[...] / l2)
    ms = jnp.mean(o_t * o_t, axis=0, keepdims=True)
    o = (o_t * lax.rsqrt(ms + LN_EPS)).T
    o_ref[...] = (o * g_ref[...] * (1.0 - lambda_init)).astype(BF16)


def _diff_attn(qkv, qkvc, lam_vec, subln_g, lambda_init, tq, tk):
    l = qkv.shape[0]
    lc = qkvc.shape[0]
    kern = functools.partial(_diff_attn_kernel, tk=tk, lambda_init=lambda_init)
    return pl.pallas_call(
        kern,
        grid=(DA_HEADS, l // tq),
        in_specs=[
            pl.BlockSpec((tq, 256), lambda h, i: (i, QD_COL // 2 + h)),
            pl.BlockSpec((l, 256), lambda h, i: (0, KD_COL // 2 + h)),
            pl.BlockSpec((l, 256), lambda h, i: (0, VD_COL // 2 + h)),
            pl.BlockSpec((lc, 256), lambda h, i: (0, KD_COL // 2 + h)),
            pl.BlockSpec((lc, 256), lambda h, i: (0, VD_COL // 2 + h)),
            pl.BlockSpec((4, HEAD_DIM), lambda h, i: (0, 0)),
            pl.BlockSpec((1, 2 * HEAD_DIM), lambda h, i: (0, 0)),
        ],
        out_specs=pl.BlockSpec((tq, 256), lambda h, i: (i, h)),
        out_shape=jax.ShapeDtypeStruct((l, DA_WIDTH), BF16),
        scratch_shapes=[pltpu.VMEM((l // tk, 256, tk), BF16),
                        pltpu.VMEM((256, tq), F32), pltpu.VMEM((256, tq), F32)]
        + [pltpu.VMEM((tk, tq), F32)] * 4,
        compiler_params=_params(("parallel", "arbitrary")),
        name="diff_attn",
    )(qkv, qkv, qkv, qkvc, qkvc, lam_vec, subln_g)


NA_ROWS = 4
NA_TQ = NA_ROWS * GRID_W
NA_SLAB = NA_ROWS + NA_KH - 1
NA_SK = NA_SLAB * GRID_W


def _na_plan(l):
    rows = l // GRID_W
    kh = min(NA_KH, rows)
    assert kh == NA_KH and rows >= NA_SLAB and rows % NA_ROWS == 0
    nblk = rows // NA_ROWS
    starts, pats, geoms = [], [], []
    for b in range(nblk):
        r0 = b * NA_ROWS
        kstart = int(np.clip(r0 - kh // 2, 0, rows - NA_SLAB))
        rs = [int(np.clip(r0 + qr - kh // 2, 0, rows - kh)) for qr in range(NA_ROWS)]
        geom = (r0 - kstart, tuple(x - kstart for x in rs))
        assert all(0 <= x and x + kh <= NA_SLAB for x in geom[1])
        if geom not in geoms:
            geoms.append(geom)
        starts.append(kstart * GRID_W)
        pats.append(geoms.index(geom))
    return np.asarray(starts, np.int32), np.asarray(pats, np.int32), geoms


def _na_bias_table(rpb, geoms):
    c = np.arange(GRID_W)
    dc = c[None, :] - c[:, None] + (NA_KW - 1)
    cs = np.clip(c - NA_KW // 2, 0, GRID_W - NA_KW)
    col_ok = (c[None, :] >= cs[:, None]) & (c[None, :] < cs[:, None] + NA_KW)
    e_col = np.zeros((2 * NA_KW - 1, GRID_W, GRID_W), np.float32)
    for ci in range(GRID_W):
        for ki in range(GRID_W):
            if col_ok[ci, ki]:
                e_col[dc[ci, ki], ci, ki] = 1.0
    npat = len(geoms)
    e_row = np.zeros((npat, NA_ROWS, NA_SLAB, 2 * NA_KH - 1), np.float32)
    row_ok = np.zeros((npat, NA_ROWS, NA_SLAB), bool)
    for p, (dq, rs) in enumerate(geoms):
        for qr in range(NA_ROWS):
            for j in range(NA_SLAB):
                if rs[qr] <= j < rs[qr] + NA_KH:
                    e_row[p, qr, j, j - (dq + qr) + (NA_KH - 1)] = 1.0
                    row_ok[p, qr, j] = True
    hi = lax.Precision.HIGHEST
    t = jnp.einsum('hrd,dck->hrck', rpb.astype(F32), jnp.asarray(e_col), precision=hi)
    bias = jnp.einsum('pqjr,hrck->phqcjk', jnp.asarray(e_row), t, precision=hi) * LOG2E
    ok = row_ok[:, None, :, None, :, None] & col_ok[None, None, None, :, None, :]
    bias = jnp.where(jnp.asarray(ok), bias, MASK_VALUE)
    return bias.reshape(npat, NA_HEADS, NA_TQ, NA_SK)


def _na_attn_kernel(start_ref, pat_ref, q_ref, k_ref, v_ref, kc_ref, vc_ref, bias_ref, o_ref):
    del pat_ref
    i = pl.program_id(1)
    off = pl.multiple_of(start_ref[i], GRID_W)
    q = q_ref[...]
    s_loc = _dot_nt(q, k_ref[pl.ds(off, NA_SK), :]) + bias_ref[...]
    s_ctx = _dot_nt(q, kc_ref[...])
    m = jnp.maximum(jnp.max(s_loc, axis=-1, keepdims=True), jnp.max(s_ctx, axis=-1, keepdims=True))
    p_loc = jnp.exp2(s_loc - m)
    p_ctx = jnp.exp2(s_ctx - m)
    l = jnp.sum(p_loc, axis=-1, keepdims=True) + jnp.sum(p_ctx, axis=-1, keepdims=True)
    o = _dot(p_loc.astype(BF16), v_ref[pl.ds(off, NA_SK), :]) + _dot(p_ctx.astype(BF16), vc_ref[...])
    o_ref[...] = (o / l).astype(BF16)


def _na_attn(qkv, qkvc, bias, starts, pats):
    l = qkv.shape[0]
    lc = qkvc.shape[0]
    grid_spec = pltpu.PrefetchScalarGridSpec(
        num_scalar_prefetch=2,
        grid=(NA_HEADS, l // NA_TQ),
        in_specs=[
            pl.BlockSpec((NA_TQ, HEAD_DIM), lambda h, i, st, pt: (i, QN_COL + h)),
            pl.BlockSpec((l, HEAD_DIM), lambda h, i, st, pt: (0, KN_COL + h)),
            pl.BlockSpec((l, HEAD_DIM), lambda h, i, st, pt: (0, VN_COL + h)),
            pl.BlockSpec((lc, HEAD_DIM), lambda h, i, st, pt: (0, KN_COL + h)),
            pl.BlockSpec((lc, HEAD_DIM), lambda h, i, st, pt: (0, VN_COL + h)),
            pl.BlockSpec((None, None, NA_TQ, NA_SK), lambda h, i, st, pt: (pt[i], h, 0, 0)),
        ],
        out_specs=pl.BlockSpec((NA_TQ, HEAD_DIM), lambda h, i, st, pt: (i, h)),
    )
    return pl.pallas_call(
        _na_attn_kernel,
        grid_spec=grid_spec,
        out_shape=jax.ShapeDtypeStruct((l, NA_WIDTH), BF16),
        compiler_params=_params(("parallel", "arbitrary")),
        name="na_attn",
    )(starts, pats, qkv, qkv, qkv, qkvc, qkvc, bias)


def _softmax_pv(q, k, v):
    s = _dot_nt(q, k)
    m = jnp.max(s, axis=-1, keepdims=True)
    p = jnp.exp2(s - m)
    l = jnp.sum(p, axis=-1, keepdims=True)
    return _dot(p.astype(BF16), v) / l


def _ctx_attn_kernel(x_ref, lam_ref, g_ref, od_ref, on_ref, *, lambda_init):
    lam = _lambda_value(lam_ref[...], lambda_init)
    g = g_ref[...]

    def col(base, h, width=HEAD_DIM):
        lo = (base + h) * HEAD_DIM
        return x_ref[:, lo:lo + width]

    for h in range(DA_HEADS):
        v = col(VD_COL, 2 * h, 2 * HEAD_DIM)
        o1 = _softmax_pv(col(QD_COL, 2 * h), col(KD_COL, 2 * h), v)
        o2 = _softmax_pv(col(QD_COL, 2 * h + 1), col(KD_COL, 2 * h + 1), v)
        od_ref[:, h * 256:(h + 1) * 256] = _subln(o1 - lam * o2, g, lambda_init).astype(BF16)
    for h in range(NA_HEADS):
        o = _softmax_pv(col(QN_COL, h), col(KN_COL, h), col(VN_COL, h))
        on_ref[:, h * HEAD_DIM:(h + 1) * HEAD_DIM] = o.astype(BF16)


def _ctx_attn(qkvc, lam_vec, subln_g, lambda_init):
    lc = qkvc.shape[0]
    kern = functools.partial(_ctx_attn_kernel, lambda_init=lambda_init)
    return pl.pallas_call(
        kern,
        grid=(1,),
        in_specs=[
            pl.BlockSpec((lc, IN_DIM), lambda i: (0, 0)),
            pl.BlockSpec((4, HEAD_DIM), lambda i: (0, 0)),
            pl.BlockSpec((1, 2 * HEAD_DIM), lambda i: (0, 0)),
        ],
        out_specs=[pl.BlockSpec((lc, DA_WIDTH), lambda i: (0, 0)),
                   pl.BlockSpec((lc, NA_WIDTH), lambda i: (0, 0))],
        out_shape=[jax.ShapeDtypeStruct((lc, DA_WIDTH), BF16),
                   jax.ShapeDtypeStruct((lc, NA_WIDTH), BF16)],
        compiler_params=_params(("arbitrary",)),
        name="ctx_attn",
    )(qkvc, lam_vec, subln_g)


def _out_proj_kernel(od_ref, on_ref, w_ref, x_ref, gate_ref, g_ref, b_ref, o_ref):
    y = _dot(od_ref[...], w_ref[0:DA_WIDTH, :]) + _dot(on_ref[...], w_ref[DA_WIDTH:, :])
    z = ALPHA * x_ref[...] + gate_ref[...] * y
    o_ref[...] = _layernorm_rows(z, g_ref[...], b_ref[...])


def _out_proj_ln(od, on, w, x, gate, ln_g, ln_b, tm):
    m = x.shape[0]
    vec = pl.BlockSpec((1, D_MODEL), lambda i: (0, 0))
    return pl.pallas_call(
        _out_proj_kernel,
        grid=(m // tm,),
        in_specs=[
            pl.BlockSpec((tm, DA_WIDTH), lambda i: (i, 0)),
            pl.BlockSpec((tm, NA_WIDTH), lambda i: (i, 0)),
            pl.BlockSpec((DA_WIDTH + NA_WIDTH, D_MODEL), lambda i: (0, 0)),
            pl.BlockSpec((tm, D_MODEL), lambda i: (i, 0)),
            vec, vec, vec,
        ],
        out_specs=pl.BlockSpec((tm, D_MODEL), lambda i: (i, 0)),
        out_shape=jax.ShapeDtypeStruct((m, D_MODEL), F32),
        compiler_params=_params(("parallel",)),
        name="out_proj_ln",
    )(od, on, w, x, gate, ln_g, ln_b)


FFN_TF = 512
HALO = 16


def _ffn_kernel(x_ref, xp_ref, xn_ref, sc_ref, sh_ref, gate_ref, wg_ref, wu_ref, cw_ref, cb_ref, wd_ref,
                g_ref, b_ref, o_ref, h_ref, acc_ref):
    i = pl.program_id(0)
    c = pl.program_id(1)
    tm = x_ref.shape[0]
    n = tm + 2 * HALO

    @pl.when(c == 0)
    def _():
        sc = 1.0 + sc_ref[...]
        sh = sh_ref[...]
        before = jnp.where(i == 0, 0.0, xp_ref[...] * sc + sh)
        after = jnp.where(i == pl.num_programs(0) - 1, 0.0, xn_ref[...] * sc + sh)
        h_ref[0:HALO, :] = before.astype(BF16)
        h_ref[HALO:HALO + tm, :] = (x_ref[...] * sc + sh).astype(BF16)
        h_ref[HALO + tm:n, :] = after.astype(BF16)
        acc_ref[...] = jnp.zeros_like(acc_ref)

    g_ext = _dot(h_ref[...], wg_ref[...])
    u = _dot(h_ref[HALO:HALO + tm, :], wu_ref[...])
    g = g_ext[HALO:HALO + tm, :]
    g_prev = pltpu.roll(g_ext, 1, 0)[HALO:HALO + tm, :]
    g_next = pltpu.roll(g_ext, n - 1, 0)[HALO:HALO + tm, :]
    gc = g_prev * cw_ref[0:1, :] + g * cw_ref[1:2, :] + g_next * cw_ref[2:3, :] + cb_ref[...]
    act = gc / (1.0 + jnp.exp(-gc)) * u
    acc_ref[...] += _dot(act.astype(BF16), wd_ref[...])

    @pl.when(c == pl.num_programs(1) - 1)
    def _():
        z = ALPHA * x_ref[...] + gate_ref[...] * acc_ref[...]
        o_ref[...] = _layernorm_rows(z, g_ref[...], b_ref[...])


def _ffn_ln(x, sc, sh, gate, w_up, conv_w, conv_b, w_down, ln_g, ln_b, tm):
    m = x.shape[0]
    nf = D_FF // FFN_TF
    tb = tm // HALO
    last_blk = m // HALO - 1
    vec = pl.BlockSpec((1, D_MODEL), lambda i, c: (0, 0))
    return pl.pallas_call(
        _ffn_kernel,
        grid=(m // tm, nf),
        in_specs=[
            pl.BlockSpec((tm, D_MODEL), lambda i, c: (i, 0)),
            pl.BlockSpec((HALO, D_MODEL), lambda i, c: (jnp.maximum(i * tb - 1, 0), 0)),
            pl.BlockSpec((HALO, D_MODEL), lambda i, c: (jnp.minimum((i + 1) * tb, last_blk), 0)),
            vec, vec, vec,
            pl.BlockSpec((D_MODEL, FFN_TF), lambda i, c: (0, c)),
            pl.BlockSpec((D_MODEL, FFN_TF), lambda i, c: (0, nf + c)),
            pl.BlockSpec((3, FFN_TF), lambda i, c: (0, c)),
            pl.BlockSpec((1, FFN_TF), lambda i, c: (0, c)),
            pl.BlockSpec((FFN_TF, D_MODEL), lambda i, c: (c, 0)),
            vec, vec,
        ],
        out_specs=pl.BlockSpec((tm, D_MODEL), lambda i, c: (i, 0)),
        out_shape=jax.ShapeDtypeStruct((m, D_MODEL), F32),
        scratch_shapes=[pltpu.VMEM((tm + 2 * HALO, D_MODEL), BF16),
                        pltpu.VMEM((tm, D_MODEL), F32)],
        compiler_params=_params(("parallel", "arbitrary")),
        name="ffn_ln",
    )(x, x, x, sc, sh, gate, w_up, w_up, conv_w, conv_b, w_down, ln_g, ln_b)


def _rope_tables(l):
    t = jnp.arange(l, dtype=jnp.int32)
    pos_r = (t // GRID_W).astype(F32)
    pos_c = (t % GRID_W).astype(F32)
    half = HEAD_DIM // 2
    inv_freq = 1.0 / (ROPE_THETA ** (jnp.arange(0, half, 2, dtype=F32) / half))
    ar = pos_r[:, None] * inv_freq[None, :]
    ac = pos_c[:, None] * inv_freq[None, :]
    ang = jnp.concatenate([ar, ar, ac, ac], axis=-1)
    cos, sin = jnp.cos(ang), jnp.sin(ang)
    first = (np.arange(HEAD_DIM) % (HEAD_DIM // 2)) < (HEAD_DIM // 4)
    sin_a = jnp.where(jnp.asarray(first)[None, :], -sin, 0.0)
    sin_b = jnp.where(jnp.asarray(first)[None, :], 0.0, sin)
    return cos, sin_a, sin_b


def kernel(x, c, ctx, c_ctx, w_ada, b_ada, w_in, da_lambda, da_subln, na_rpb, w_o,
           ln1_g, ln1_b, w_up, conv_w, conv_b, w_down, ln2_g, ln2_b):
    assert x.shape[0] == 1 and ctx.shape[0] == 1
    l = x.shape[1]
    lc = ctx.shape[1]
    xs = x[0]
    xc = ctx[0]

    cc = jnp.zeros((8, D_MODEL), F32).at[0].set(c[0]).at[1].set(c_ctx)
    mods = _ada_mod(cc, w_ada, b_ada)

    cos, sin_a, sin_b = _rope_tables(l)
    ones_c = jnp.ones((lc, HEAD_DIM), F32)
    zeros_c = jnp.zeros((lc, HEAD_DIM), F32)
    starts, pats, geoms = _na_plan(l)
    starts = jnp.asarray(starts)
    pats = jnp.asarray(pats)

    for layer in range(DEPTH):
        last = layer == DEPTH - 1
        lambda_init = 0.8 - 0.6 * math.exp(-0.3 * layer)
        sh_a, sc_a, g_a, sh_m, sc_m, g_m = jnp.split(mods[layer, 0:1], N_MOD, axis=-1)
        shc_a, scc_a, gc_a, shc_m, scc_m, gc_m = jnp.split(mods[layer, 1:2], N_MOD, axis=-1)
        w_in_l = w_in[layer].astype(BF16)
        w_o_l = w_o[layer].astype(BF16)
        w_up_l = w_up[layer].astype(BF16)
        w_down_l = w_down[layer].astype(BF16)
        lam_vec = da_lambda[layer]
        subln_g = da_subln[layer].reshape(1, 2 * HEAD_DIM)
        ln1 = (ln1_g[layer].reshape(1, D_MODEL), ln1_b[layer].reshape(1, D_MODEL))
        ln2 = (ln2_g[layer].reshape(1, D_MODEL), ln2_b[layer].reshape(1, D_MODEL))
        cw = conv_w[layer]
        cb = conv_b[layer].reshape(1, D_FF)

        qkv = _in_proj(xs, sc_a, sh_a, w_in_l, cos, sin_a, sin_b, tm=256)
        qkvc = _in_proj(xc, scc_a, shc_a, w_in_l, ones_c, zeros_c, zeros_c, tm=lc)

        od = _diff_attn(qkv, qkvc, lam_vec, subln_g, lambda_init, tq=512, tk=512)
        bias = _na_bias_table(na_rpb[layer], geoms)
        on = _na_attn(qkv, qkvc, bias, starts, pats)

        xs = _out_proj_ln(od, on, w_o_l, xs, g_a, *ln1, tm=256)
        xs = _ffn_ln(xs, sc_m, sh_m, g_m, w_up_l, cw, cb, w_down_l, *ln2, tm=512)

        if not last:
            odc, onc = _ctx_attn(qkvc, lam_vec, subln_g, lambda_init)
            xc = _out_proj_ln(odc, onc, w_o_l, xc, gc_a, *ln1, tm=lc)
            xc = _ffn_ln(xc, scc_m, shc_m, gc_m, w_up_l, cw, cb, w_down_l, *ln2, tm=lc)

    return xs[None]
```

```python
import functools
import math

import numpy as np
import jax
import jax.numpy as jnp
from jax import lax
from jax.experimental import pallas as pl
from jax.experimental.pallas import tpu as pltpu

D_MODEL = 2048
DEPTH = 2
GRID_W = 64
HEAD_DIM = 128
DA_HEADS = 4
NA_HEADS = 8
DA_WIDTH = DA_HEADS * 2 * HEAD_DIM
NA_WIDTH = NA_HEADS * HEAD_DIM
IN_DIM = 3 * DA_WIDTH + 3 * NA_WIDTH
NA_KH = 8
NA_KW = 16
D_FF = 5632
ROPE_THETA = 10000.0
LN_EPS = 1e-5
N_MOD = 6
ALPHA = (2.0 * DEPTH) ** 0.25
LOG2E = math.log2(math.e)
Q_SCALE = (HEAD_DIM ** -0.5) * LOG2E
MASK_VALUE = -1e30

F32 = jnp.float32
BF16 = jnp.bfloat16

VMEM_LIMIT = 56 * 1024 * 1024

QD_COL, KD_COL, VD_COL = 0, DA_WIDTH // 128, 2 * DA_WIDTH // 128
QN_COL = 3 * DA_WIDTH // 128
KN_COL = QN_COL + NA_WIDTH // 128
VN_COL = KN_COL + NA_WIDTH // 128


def _params(semantics):
    return pltpu.CompilerParams(dimension_semantics=semantics, vmem_limit_bytes=VMEM_LIMIT)


def _dot(a, b):
    return jnp.dot(a, b, preferred_element_type=F32)


def _dot_nt(a, b):
    return lax.dot_general(a, b, (((1,), (1,)), ((), ())), preferred_element_type=F32)


def _layernorm_rows(z, g, b):
    mu = jnp.mean(z, axis=-1, keepdims=True)
    zc = z - mu
    var = jnp.mean(zc * zc, axis=-1, keepdims=True)
    return zc * lax.rsqrt(var + LN_EPS) * g + b


ADA_TN = 1024
ADA_KC = 256


def _ada_kernel(c_ref, w_ref, b_ref, o_ref):
    reps = w_ref.shape[1] // 128
    for r in range(c_ref.shape[0]):
        acc = b_ref[...]
        for k0 in range(0, D_MODEL, ADA_KC):
            a = c_ref[r, k0:k0 + ADA_KC, :]
            silu = a / (1.0 + jnp.exp(-a))
            sw = jnp.concatenate([silu] * reps, axis=1)
            acc = acc + jnp.sum(w_ref[k0:k0 + ADA_KC, :] * sw, axis=0, keepdims=True)
        o_ref[r:r + 1, :] = acc


def _ada_mod(cond, w_ada, b_ada):
    nmod = N_MOD * D_MODEL
    nrow = cond.shape[0]
    return pl.pallas_call(
        _ada_kernel,
        grid=(DEPTH, nmod // ADA_TN),
        in_specs=[
            pl.BlockSpec((nrow, D_MODEL, 128), lambda l, j: (0, 0, 0)),
            pl.BlockSpec((None, D_MODEL, ADA_TN), lambda l, j: (l, 0, j)),
            pl.BlockSpec((None, 1, ADA_TN), lambda l, j: (l, 0, j)),
        ],
        out_specs=pl.BlockSpec((None, nrow, ADA_TN), lambda l, j: (l, 0, j)),
        out_shape=jax.ShapeDtypeStruct((DEPTH, nrow, nmod), F32),
        compiler_params=_params(("parallel", "parallel")),
        name="ada_mod",
    )(cond, w_ada, b_ada.reshape(DEPTH, 1, nmod))


PROJ_TN = 1024


def _in_proj_kernel(x_ref, sc_ref, sh_ref, w_ref, cos_ref, sa_ref, sb_ref, o_ref, h_ref):
    h_ref[...] = (x_ref[...] * (1.0 + sc_ref[...]) + sh_ref[...]).astype(BF16)
    for j in range(IN_DIM // PROJ_TN):
        acc = _dot(h_ref[...], w_ref[:, j * PROJ_TN:(j + 1) * PROJ_TN])
        if j <= 1:
            scale = Q_SCALE if j == 0 else 1.0
            cos = cos_ref[...] * scale
            sa = sa_ref[...] * scale
            sb = sb_ref[...] * scale
            for k in range(PROJ_TN // HEAD_DIM):
                a = acc[:, k * HEAD_DIM:(k + 1) * HEAD_DIM]
                r = a * cos + pltpu.roll(a, 96, 1) * sa + pltpu.roll(a, 32, 1) * sb
                o_ref[:, j * PROJ_TN + k * HEAD_DIM:j * PROJ_TN + (k + 1) * HEAD_DIM] = r.astype(BF16)
        elif j == 3:
            o_ref[:, j * PROJ_TN:(j + 1) * PROJ_TN] = (acc * Q_SCALE).astype(BF16)
        else:
            o_ref[:, j * PROJ_TN:(j + 1) * PROJ_TN] = acc.astype(BF16)


def _in_proj(x, sc, sh, w, cos, sa, sb, tm):
    m = x.shape[0]
    return pl.pallas_call(
        _in_proj_kernel,
        grid=(m // tm,),
        in_specs=[
            pl.BlockSpec((tm, D_MODEL), lambda i: (i, 0)),
            pl.BlockSpec((1, D_MODEL), lambda i: (0, 0)),
            pl.BlockSpec((1, D_MODEL), lambda i: (0, 0)),
            pl.BlockSpec((D_MODEL, IN_DIM), lambda i: (0, 0), pipeline_mode=pl.Buffered(1)),
            pl.BlockSpec((tm, HEAD_DIM), lambda i: (i, 0)),
            pl.BlockSpec((tm, HEAD_DIM), lambda i: (i, 0)),
            pl.BlockSpec((tm, HEAD_DIM), lambda i: (i, 0)),
        ],
        out_specs=pl.BlockSpec((tm, IN_DIM), lambda i: (i, 0)),
        out_shape=jax.ShapeDtypeStruct((m, IN_DIM), BF16),
        scratch_shapes=[pltpu.VMEM((tm, D_MODEL), BF16)],
        compiler_params=_params(("parallel",)),
        name="in_proj",
    )(x, sc, sh, w, cos, sa, sb)


UNROLL = 8


def _lambda_value(lv, lambda_init):
    a = jnp.sum(lv[0:1, :] * lv[1:2, :], axis=-1, keepdims=True)
    b = jnp.sum(lv[2:3, :] * lv[3:4, :], axis=-1, keepdims=True)
    return jnp.exp(a) - jnp.exp(b) + lambda_init


def _subln(o, g, lambda_init):
    ms = jnp.mean(o * o, axis=-1, keepdims=True)
    return o * lax.rsqrt(ms + LN_EPS) * g * (1.0 - lambda_init)


def _diff_attn_kernel(q_ref, k_ref, v_ref, kc_ref, vc_ref, lam_ref, g_ref, o_ref,
                      vt_ref, acc1_ref, acc2_ref, sa1_ref, sa2_ref, sb1_ref, sb2_ref, *, tk, lambda_init):
    nk = k_ref.shape[0] // tk
    accs = (acc1_ref, acc2_ref)
    buf_a = (sa1_ref, sa2_ref)
    buf_b = (sb1_ref, sb2_ref)

    @pl.when(pl.program_id(1) == 0)
    def _():
        def fill(i, carry):
            off = pl.multiple_of(i * tk, tk)
            vt_ref[i] = v_ref[pl.ds(off, tk), :].astype(F32).T.astype(BF16)
            return carry
        lax.fori_loop(0, nk, fill, 0)

    q_t = q_ref[...].astype(F32).T.astype(BF16)
    qs = (q_t[0:HEAD_DIM, :], q_t[HEAD_DIM:2 * HEAD_DIM, :])

    def scores_into(i, dst):
        off = pl.multiple_of(i * tk, tk)
        mx = []
        for c in range(2):
            s = _dot(k_ref[pl.ds(off, tk), c * HEAD_DIM:(c + 1) * HEAD_DIM], qs[c])
            dst[c][...] = s
            mx.append(jnp.max(s, axis=0, keepdims=True))
        return tuple(mx)

    def softmax_pv(i, src, mx, carry):
        vt = vt_ref[i]
        out = []
        for c in range(2):
            m, l = carry[2 * c], carry[2 * c + 1]
            mn = jnp.maximum(m, mx[c])
            a = jnp.exp2(m - mn)
            p = jnp.exp2(src[c][...] - mn)
            l = a * l + jnp.sum(p, axis=0, keepdims=True)
            accs[c][...] = a * accs[c][...] + _dot(vt, p.astype(BF16))
            out += [mn, l]
        return tuple(out)

    mx_a = scores_into(0, buf_a)

    vc_t = vc_ref[...].astype(F32).T.astype(BF16)
    carry = []
    for c in range(2):
        s = _dot(kc_ref[:, c * HEAD_DIM:(c + 1) * HEAD_DIM], qs[c])
        m = jnp.max(s, axis=0, keepdims=True)
        p = jnp.exp2(s - m)
        l = jnp.sum(p, axis=0, keepdims=True)
        accs[c][...] = _dot(vc_t, p.astype(BF16))
        carry += [m, l]

    bufs = (buf_a, buf_b)
    unroll = math.gcd(nk, UNROLL)
    assert unroll % 2 == 0

    def blocks(base, mx, carry, produce_last):
        for u in range(unroll):
            mx_next = None
            if produce_last or u < unroll - 1:
                mx_next = scores_into(base + u + 1, bufs[(u + 1) % 2])
            carry = softmax_pv(base + u, bufs[u % 2], mx, carry)
            mx = mx_next
        return mx, carry

    def body(j, state):
        return blocks(unroll * j, state[0], state[1], True)

    mx_a, carry = lax.fori_loop(0, nk // unroll - 1, body, (mx_a, tuple(carry)))
    _, (m1, l1, m2, l2) = blocks(nk - unroll, mx_a, carry, False)

    lam = _lambda_value(lam_ref[...], lambda_init)
    o_t = acc1_ref[...] / l1 - lam * (acc2_ref[...] / l2)
    ms = jnp.mean(o_t * o_t, axis=0, keepdims=True)
    o = (o_t * lax.rsqrt(ms + LN_EPS)).T
    o_ref[...] = (o * g_ref[...] * (1.0 - lambda_init)).astype(BF16)


def _diff_attn(qkv, qkvc, lam_vec, subln_g, lambda_init, tq, tk):
    l = qkv.shape[0]
    lc = qkvc.shape[0]
    kern = functools.partial(_diff_attn_kernel, tk=tk, lambda_init=lambda_init)
    return pl.pallas_call(
        kern,
        grid=(DA_HEADS, l // tq),
        in_specs=[
            pl.BlockSpec((tq, 256), lambda h, i: (i, QD_COL // 2 + h)),
            pl.BlockSpec((l, 256), lambda h, i: (0, KD_COL // 2 + h)),
            pl.BlockSpec((l, 256), lambda h, i: (0, VD_COL // 2 + h)),
            pl.BlockSpec((lc, 256), lambda h, i: (0, KD_COL // 2 + h)),
            pl.BlockSpec((lc, 256), lambda h, i: (0, VD_COL // 2 + h)),
            pl.BlockSpec((4, HEAD_DIM), lambda h, i: (0, 0)),
            pl.BlockSpec((1, 2 * HEAD_DIM), lambda h, i: (0, 0)),
        ],
        out_specs=pl.BlockSpec((tq, 256), lambda h, i: (i, h)),
        out_shape=jax.ShapeDtypeStruct((l, DA_WIDTH), BF16),
        scratch_shapes=[pltpu.VMEM((l // tk, 256, tk), BF16),
                        pltpu.VMEM((256, tq), F32), pltpu.VMEM((256, tq), F32)]
        + [pltpu.VMEM((tk, tq), F32)] * 4,
        compiler_params=_params(("parallel", "arbitrary")),
        name="diff_attn",
    )(qkv, qkv, qkv, qkvc, qkvc, lam_vec, subln_g)


NA_ROWS = 4
NA_TQ = NA_ROWS * GRID_W
NA_SLAB = NA_ROWS + NA_KH - 1
NA_SK = NA_SLAB * GRID_W


def _na_plan(l):
    rows = l // GRID_W
    kh = min(NA_KH, rows)
    assert kh == NA_KH and rows >= NA_SLAB and rows % NA_ROWS == 0
    nblk = rows // NA_ROWS
    starts, pats, geoms = [], [], []
    for b in range(nblk):
        r0 = b * NA_ROWS
        kstart = int(np.clip(r0 - kh // 2, 0, rows - NA_SLAB))
        rs = [int(np.clip(r0 + qr - kh // 2, 0, rows - kh)) for qr in range(NA_ROWS)]
        geom = (r0 - kstart, tuple(x - kstart for x in rs))
        assert all(0 <= x and x + kh <= NA_SLAB for x in geom[1])
        if geom not in geoms:
            geoms.append(geom)
        starts.append(kstart * GRID_W)
        pats.append(geoms.index(geom))
    return np.asarray(starts, np.int32), np.asarray(pats, np.int32), geoms


def _na_bias_kernel(t2_ref, o_ref, *, geoms):
    p = pl.program_id(1)
    lane = lax.broadcasted_iota(jnp.int32, (GRID_W, 2 * GRID_W), 1)
    for ps, (dq, rs) in enumerate(geoms):
        @pl.when(p == ps)
        def _(dq=dq, rs=rs):
            for qr in range(NA_ROWS):
                def inside(j, qr=qr):
                    return j < NA_SLAB and rs[qr] <= j < rs[qr] + NA_KH
                for j0 in range(0, NA_SLAB, 2):
                    left, right = inside(j0), inside(j0 + 1)
                    d0 = j0 - (dq + qr) + (NA_KH - 1)
                    if left or right:
                        blk = t2_ref[d0 + 1]
                        if not left:
                            blk = jnp.where(lane < GRID_W, MASK_VALUE, blk)
                        if not right:
                            blk = jnp.where(lane < GRID_W, blk, MASK_VALUE)
                    else:
                        blk = jnp.full((GRID_W, 2 * GRID_W), MASK_VALUE, F32)
                    width = min(2, NA_SLAB - j0) * GRID_W
                    o_ref[qr * GRID_W:(qr + 1) * GRID_W, j0 * GRID_W:j0 * GRID_W + width] = blk[:, :width]


def _na_bias_table(rpb, geoms):
    c = np.arange(GRID_W)
    dc = c[None, :] - c[:, None] + (NA_KW - 1)
    cs = np.clip(c - NA_KW // 2, 0, GRID_W - NA_KW)
    col_ok = (c[None, :] >= cs[:, None]) & (c[None, :] < cs[:, None] + NA_KW)
    e_col = np.zeros((2 * NA_KW - 1, GRID_W, GRID_W), np.float32)
    for ci in range(GRID_W):
        for ki in range(GRID_W):
            if col_ok[ci, ki]:
                e_col[dc[ci, ki], ci, ki] = 1.0
    t = jnp.einsum('hrd,dck->hrck', rpb.astype(F32), jnp.asarray(e_col), precision=lax.Precision.HIGHEST)
    t = jnp.where(jnp.asarray(col_ok)[None, None], t * LOG2E, MASK_VALUE)
    tp = jnp.pad(t, ((0, 0), (1, 1), (0, 0), (0, 0)))
    t2 = jnp.concatenate([tp[:, :-1], tp[:, 1:]], axis=-1)
    npat = len(geoms)
    return pl.pallas_call(
        functools.partial(_na_bias_kernel, geoms=tuple(geoms)),
        grid=(NA_HEADS, npat),
        in_specs=[pl.BlockSpec((None, 2 * NA_KH, GRID_W, 2 * GRID_W), lambda h, p: (h, 0, 0, 0))],
        out_specs=pl.BlockSpec((None, None, NA_TQ, NA_SK), lambda h, p: (p, h, 0, 0)),
        out_shape=jax.ShapeDtypeStruct((npat, NA_HEADS, NA_TQ, NA_SK), F32),
        compiler_params=_params(("parallel", "arbitrary")),
        name="na_bias",
    )(t2)


NA_HPS = 4


def _na_attn_kernel(start_ref, pat_ref, q_ref, k_ref, v_ref, kc_ref, vc_ref, bias_ref, o_ref):
    del pat_ref
    i = pl.program_id(1)
    off = pl.multiple_of(start_ref[i], GRID_W)
    for h in range(NA_HPS):
        cols = slice(h * HEAD_DIM, (h + 1) * HEAD_DIM)
        q = q_ref[:, cols]
        s_loc = _dot_nt(q, k_ref[pl.ds(off, NA_SK), cols]) + bias_ref[h]
        s_ctx = _dot_nt(q, kc_ref[:, cols])
        m = jnp.maximum(jnp.max(s_loc, axis=-1, keepdims=True), jnp.max(s_ctx, axis=-1, keepdims=True))
        p_loc = jnp.exp2(s_loc - m)
        p_ctx = jnp.exp2(s_ctx - m)
        l = jnp.sum(p_loc, axis=-1, keepdims=True) + jnp.sum(p_ctx, axis=-1, keepdims=True)
        o = (_dot(p_loc.astype(BF16), v_ref[pl.ds(off, NA_SK), cols])
             + _dot(p_ctx.astype(BF16), vc_ref[:, cols]))
        o_ref[:, cols] = (o / l).astype(BF16)


def _na_attn(qkv, qkvc, bias, starts, pats):
    l = qkv.shape[0]
    lc = qkvc.shape[0]
    w = NA_HPS * HEAD_DIM
    once = pl.Buffered(1)
    grid_spec = pltpu.PrefetchScalarGridSpec(
        num_scalar_prefetch=2,
        grid=(NA_HEADS // NA_HPS, l // NA_TQ),
        in_specs=[
            pl.BlockSpec((NA_TQ, w), lambda h, i, st, pt: (i, QN_COL // NA_HPS + h)),
            pl.BlockSpec((l, w), lambda h, i, st, pt: (0, KN_COL // NA_HPS + h), pipeline_mode=once),
            pl.BlockSpec((l, w), lambda h, i, st, pt: (0, VN_COL // NA_HPS + h), pipeline_mode=once),
            pl.BlockSpec((lc, w), lambda h, i, st, pt: (0, KN_COL // NA_HPS + h)),
            pl.BlockSpec((lc, w), lambda h, i, st, pt: (0, VN_COL // NA_HPS + h)),
            pl.BlockSpec((None, NA_HPS, NA_TQ, NA_SK), lambda h, i, st, pt: (pt[i], h, 0, 0)),
        ],
        out_specs=pl.BlockSpec((NA_TQ, w), lambda h, i, st, pt: (i, h)),
    )
    return pl.pallas_call(
        _na_attn_kernel,
        grid_spec=grid_spec,
        out_shape=jax.ShapeDtypeStruct((l, NA_WIDTH), BF16),
        compiler_params=_params(("parallel", "arbitrary")),
        name="na_attn",
    )(starts, pats, qkv, qkv, qkv, qkvc, qkvc, bias)


def _softmax_pv(q, k, v):
    s = _dot_nt(q, k)
    m = jnp.max(s, axis=-1, keepdims=True)
    p = jnp.exp2(s - m)
    l = jnp.sum(p, axis=-1, keepdims=True)
    return _dot(p.astype(BF16), v) / l


def _ctx_attn_kernel(x_ref, lam_ref, g_ref, od_ref, on_ref, *, lambda_init):
    lam = _lambda_value(lam_ref[...], lambda_init)
    g = g_ref[...]

    def col(base, h, width=HEAD_DIM):
        lo = (base + h) * HEAD_DIM
        return x_ref[:, lo:lo + width]

    for h in range(DA_HEADS):
        v = col(VD_COL, 2 * h, 2 * HEAD_DIM)
        o1 = _softmax_pv(col(QD_COL, 2 * h), col(KD_COL, 2 * h), v)
        o2 = _softmax_pv(col(QD_COL, 2 * h + 1), col(KD_COL, 2 * h + 1), v)
        od_ref[:, h * 256:(h + 1) * 256] = _subln(o1 - lam * o2, g, lambda_init).astype(BF16)
    for h in range(NA_HEADS):
        o = _softmax_pv(col(QN_COL, h), col(KN_COL, h), col(VN_COL, h))
        on_ref[:, h * HEAD_DIM:(h + 1) * HEAD_DIM] = o.astype(BF16)


def _ctx_attn(qkvc, lam_vec, subln_g, lambda_init):
    lc = qkvc.shape[0]
    kern = functools.partial(_ctx_attn_kernel, lambda_init=lambda_init)
    return pl.pallas_call(
        kern,
        grid=(1,),
        in_specs=[
            pl.BlockSpec((lc, IN_DIM), lambda i: (0, 0)),
            pl.BlockSpec((4, HEAD_DIM), lambda i: (0, 0)),
            pl.BlockSpec((1, 2 * HEAD_DIM), lambda i: (0, 0)),
        ],
        out_specs=[pl.BlockSpec((lc, DA_WIDTH), lambda i: (0, 0)),
                   pl.BlockSpec((lc, NA_WIDTH), lambda i: (0, 0))],
        out_shape=[jax.ShapeDtypeStruct((lc, DA_WIDTH), BF16),
                   jax.ShapeDtypeStruct((lc, NA_WIDTH), BF16)],
        compiler_params=_params(("arbitrary",)),
        name="ctx_attn",
    )(qkvc, lam_vec, subln_g)


OUT_SUB = 256


def _out_proj_kernel(od_ref, on_ref, w_ref, x_ref, gate_ref, g_ref, b_ref, o_ref):
    tm = x_ref.shape[0]
    for r0 in range(0, tm, OUT_SUB):
        rows = slice(r0, r0 + OUT_SUB)
        y = _dot(od_ref[rows, :], w_ref[0:DA_WIDTH, :]) + _dot(on_ref[rows, :], w_ref[DA_WIDTH:, :])
        z = ALPHA * x_ref[rows, :] + gate_ref[...] * y
        o_ref[rows, :] = _layernorm_rows(z, g_ref[...], b_ref[...])


def _out_proj_ln(od, on, w, x, gate, ln_g, ln_b, tm):
    m = x.shape[0]
    vec = pl.BlockSpec((1, D_MODEL), lambda i: (0, 0))
    return pl.pallas_call(
        _out_proj_kernel,
        grid=(m // tm,),
        in_specs=[
            pl.BlockSpec((tm, DA_WIDTH), lambda i: (i, 0)),
            pl.BlockSpec((tm, NA_WIDTH), lambda i: (i, 0)),
            pl.BlockSpec((DA_WIDTH + NA_WIDTH, D_MODEL), lambda i: (0, 0), pipeline_mode=pl.Buffered(1)),
            pl.BlockSpec((tm, D_MODEL), lambda i: (i, 0)),
            vec, vec, vec,
        ],
        out_specs=pl.BlockSpec((tm, D_MODEL), lambda i: (i, 0)),
        out_shape=jax.ShapeDtypeStruct((m, D_MODEL), F32),
        compiler_params=_params(("parallel",)),
        name="out_proj_ln",
    )(od, on, w, x, gate, ln_g, ln_b)


FFN_TF = 512
HALO = 16


def _ffn_kernel(x_ref, xp_ref, xn_ref, sc_ref, sh_ref, gate_ref, wg_ref, wu_ref, cw_ref, cb_ref, wd_ref,
                g_ref, b_ref, o_ref, h_ref, acc_ref):
    i = pl.program_id(0)
    c = pl.program_id(1)
    tm = x_ref.shape[0]
    n = tm + 2 * HALO

    @pl.when(c == 0)
    def _():
        sc = 1.0 + sc_ref[...]
        sh = sh_ref[...]
        before = jnp.where(i == 0, 0.0, xp_ref[...] * sc + sh)
        after = jnp.where(i == pl.num_programs(0) - 1, 0.0, xn_ref[...] * sc + sh)
        h_ref[0:HALO, :] = before.astype(BF16)
        h_ref[HALO:HALO + tm, :] = (x_ref[...] * sc + sh).astype(BF16)
        h_ref[HALO + tm:n, :] = after.astype(BF16)
        acc_ref[...] = jnp.zeros_like(acc_ref)

    g_ext = _dot(h_ref[...], wg_ref[...])
    u = _dot(h_ref[HALO:HALO + tm, :], wu_ref[...])
    g = g_ext[HALO:HALO + tm, :]
    g_prev = pltpu.roll(g_ext, 1, 0)[HALO:HALO + tm, :]
    g_next = pltpu.roll(g_ext, n - 1, 0)[HALO:HALO + tm, :]
    gc = g_prev * cw_ref[0:1, :] + g * cw_ref[1:2, :] + g_next * cw_ref[2:3, :] + cb_ref[...]
    act = gc / (1.0 + jnp.exp(-gc)) * u
    acc_ref[...] += _dot(act.astype(BF16), wd_ref[...])

    @pl.when(c == pl.num_programs(1) - 1)
    def _():
        z = ALPHA * x_ref[...] + gate_ref[...] * acc_ref[...]
        o_ref[...] = _layernorm_rows(z, g_ref[...], b_ref[...])


def _ffn_ln(x, sc, sh, gate, w_up, conv_w, conv_b, w_down, ln_g, ln_b, tm):
    m = x.shape[0]
    nf = D_FF // FFN_TF
    tb = tm // HALO
    last_blk = m // HALO - 1
    vec = pl.BlockSpec((1, D_MODEL), lambda i, c: (0, 0))
    return pl.pallas_call(
        _ffn_kernel,
        grid=(m // tm, nf),
        in_specs=[
            pl.BlockSpec((tm, D_MODEL), lambda i, c: (i, 0)),
            pl.BlockSpec((HALO, D_MODEL), lambda i, c: (jnp.maximum(i * tb - 1, 0), 0)),
            pl.BlockSpec((HALO, D_MODEL), lambda i, c: (jnp.minimum((i + 1) * tb, last_blk), 0)),
            vec, vec, vec,
            pl.BlockSpec((D_MODEL, FFN_TF), lambda i, c: (0, c)),
            pl.BlockSpec((D_MODEL, FFN_TF), lambda i, c: (0, nf + c)),
            pl.BlockSpec((3, FFN_TF), lambda i, c: (0, c)),
            pl.BlockSpec((1, FFN_TF), lambda i, c: (0, c)),
            pl.BlockSpec((FFN_TF, D_MODEL), lambda i, c: (c, 0)),
            vec, vec,
        ],
        out_specs=pl.BlockSpec((tm, D_MODEL), lambda i, c: (i, 0)),
        out_shape=jax.ShapeDtypeStruct((m, D_MODEL), F32),
        scratch_shapes=[pltpu.VMEM((tm + 2 * HALO, D_MODEL), BF16),
                        pltpu.VMEM((tm, D_MODEL), F32)],
        compiler_params=_params(("parallel", "arbitrary")),
        name="ffn_ln",
    )(x, x, x, sc, sh, gate, w_up, w_up, conv_w, conv_b, w_down, ln_g, ln_b)


def _rope_tables(l):
    t = jnp.arange(l, dtype=jnp.int32)
    pos_r = (t // GRID_W).astype(F32)
    pos_c = (t % GRID_W).astype(F32)
    half = HEAD_DIM // 2
    inv_freq = 1.0 / (ROPE_THETA ** (jnp.arange(0, half, 2, dtype=F32) / half))
    ar = pos_r[:, None] * inv_freq[None, :]
    ac = pos_c[:, None] * inv_freq[None, :]
    ang = jnp.concatenate([ar, ar, ac, ac], axis=-1)
    cos, sin = jnp.cos(ang), jnp.sin(ang)
    first = (np.arange(HEAD_DIM) % (HEAD_DIM // 2)) < (HEAD_DIM // 4)
    sin_a = jnp.where(jnp.asarray(first)[None, :], -sin, 0.0)
    sin_b = jnp.where(jnp.asarray(first)[None, :], 0.0, sin)
    return cos, sin_a, sin_b


def kernel(x, c, ctx, c_ctx, w_ada, b_ada, w_in, da_lambda, da_subln, na_rpb, w_o,
           ln1_g, ln1_b, w_up, conv_w, conv_b, w_down, ln2_g, ln2_b):
    assert x.shape[0] == 1 and ctx.shape[0] == 1
    l = x.shape[1]
    lc = ctx.shape[1]
    xs = x[0]
    xc = ctx[0]

    cond = jnp.broadcast_to(jnp.stack([c[0], c_ctx])[:, :, None], (2, D_MODEL, 128))
    mods = _ada_mod(cond, w_ada, b_ada)

    cos, sin_a, sin_b = _rope_tables(l)
    ones_c = jnp.ones((lc, HEAD_DIM), F32)
    zeros_c = jnp.zeros((lc, HEAD_DIM), F32)
    starts, pats, geoms = _na_plan(l)
    starts = jnp.asarray(starts)
    pats = jnp.asarray(pats)

    for layer in range(DEPTH):
        last = layer == DEPTH - 1
        lambda_init = 0.8 - 0.6 * math.exp(-0.3 * layer)
        sh_a, sc_a, g_a, sh_m, sc_m, g_m = jnp.split(mods[layer, 0:1], N_MOD, axis=-1)
        shc_a, scc_a, gc_a, shc_m, scc_m, gc_m = jnp.split(mods[layer, 1:2], N_MOD, axis=-1)
        w_in_l = w_in[layer].astype(BF16)
        w_o_l = w_o[layer].astype(BF16)
        w_up_l = w_up[layer].astype(BF16)
        w_down_l = w_down[layer].astype(BF16)
        lam_vec = da_lambda[layer]
        subln_g = da_subln[layer].reshape(1, 2 * HEAD_DIM)
        ln1 = (ln1_g[layer].reshape(1, D_MODEL), ln1_b[layer].reshape(1, D_MODEL))
        ln2 = (ln2_g[layer].reshape(1, D_MODEL), ln2_b[layer].reshape(1, D_MODEL))
        cw = conv_w[layer]
        cb = conv_b[layer].reshape(1, D_FF)

        qkv = _in_proj(xs, sc_a, sh_a, w_in_l, cos, sin_a, sin_b, tm=256)
        qkvc = _in_proj(xc, scc_a, shc_a, w_in_l, ones_c, zeros_c, zeros_c, tm=lc)

        od = _diff_attn(qkv, qkvc, lam_vec, subln_g, lambda_init, tq=512, tk=512)
        bias = _na_bias_table(na_rpb[layer], geoms)
        on = _na_attn(qkv, qkvc, bias, starts, pats)

        xs = _out_proj_ln(od, on, w_o_l, xs, g_a, *ln1, tm=512)
        xs = _ffn_ln(xs, sc_m, sh_m, g_m, w_up_l, cw, cb, w_down_l, *ln2, tm=512)

        if not last:
            odc, onc = _ctx_attn(qkvc, lam_vec, subln_g, lambda_init)
            xc = _out_proj_ln(odc, onc, w_o_l, xc, gc_a, *ln1, tm=lc)
            xc = _ffn_ln(xc, scc_m, shc_m, gc_m, w_up_l, cw, cb, w_down_l, *ln2, tm=lc)

    return xs[None]
```

```python
import functools
import math

import numpy as np
import jax
import jax.numpy as jnp
from jax import lax
from jax.experimental import pallas as pl
from jax.experimental.pallas import tpu as pltpu

D_MODEL = 2048
DEPTH = 2
GRID_W = 64
HEAD_DIM = 128
DA_HEADS = 4
NA_HEADS = 8
DA_WIDTH = DA_HEADS * 2 * HEAD_DIM
NA_WIDTH = NA_HEADS * HEAD_DIM
IN_DIM = 3 * DA_WIDTH + 3 * NA_WIDTH
NA_KH = 8
NA_KW = 16
D_FF = 5632
ROPE_THETA = 10000.0
LN_EPS = 1e-5
N_MOD = 6
ALPHA = (2.0 * DEPTH) ** 0.25
LOG2E = math.log2(math.e)
Q_SCALE = (HEAD_DIM ** -0.5) * LOG2E
MASK_VALUE = -1e30

F32 = jnp.float32
BF16 = jnp.bfloat16

VMEM_LIMIT = 56 * 1024 * 1024

QD_COL, KD_COL, VD_COL = 0, DA_WIDTH // 128, 2 * DA_WIDTH // 128
QN_COL = 3 * DA_WIDTH // 128
KN_COL = QN_COL + NA_WIDTH // 128
VN_COL = KN_COL + NA_WIDTH // 128


def _params(semantics):
    return pltpu.CompilerParams(dimension_semantics=semantics, vmem_limit_bytes=VMEM_LIMIT)


def _dot(a, b):
    return jnp.dot(a, b, preferred_element_type=F32)


def _dot_nt(a, b):
    return lax.dot_general(a, b, (((1,), (1,)), ((), ())), preferred_element_type=F32)


def _layernorm_rows(z, g, b):
    mu = jnp.mean(z, axis=-1, keepdims=True)
    zc = z - mu
    var = jnp.mean(zc * zc, axis=-1, keepdims=True)
    return zc * lax.rsqrt(var + LN_EPS) * g + b


ADA_TN = 1024
ADA_KC = 256


def _ada_kernel(c_ref, w_ref, b_ref, o_ref):
    reps = w_ref.shape[1] // 128
    for r in range(c_ref.shape[0]):
        acc = b_ref[...]
        for k0 in range(0, D_MODEL, ADA_KC):
            a = c_ref[r, k0:k0 + ADA_KC, :]
            silu = a / (1.0 + jnp.exp(-a))
            sw = jnp.concatenate([silu] * reps, axis=1)
            acc = acc + jnp.sum(w_ref[k0:k0 + ADA_KC, :] * sw, axis=0, keepdims=True)
        o_ref[r:r + 1, :] = acc


def _ada_mod(cond, w_ada, b_ada):
    nmod = N_MOD * D_MODEL
    nrow = cond.shape[0]
    return pl.pallas_call(
        _ada_kernel,
        grid=(DEPTH, nmod // ADA_TN),
        in_specs=[
            pl.BlockSpec((nrow, D_MODEL, 128), lambda l, j: (0, 0, 0)),
            pl.BlockSpec((None, D_MODEL, ADA_TN), lambda l, j: (l, 0, j)),
            pl.BlockSpec((None, 1, ADA_TN), lambda l, j: (l, 0, j)),
        ],
        out_specs=pl.BlockSpec((None, nrow, ADA_TN), lambda l, j: (l, 0, j)),
        out_shape=jax.ShapeDtypeStruct((DEPTH, nrow, nmod), F32),
        compiler_params=_params(("parallel", "parallel")),
        name="ada_mod",
    )(cond, w_ada, b_ada.reshape(DEPTH, 1, nmod))


PROJ_TN = 1024


def _rope_tile(row_ref, col_ref, i, tm):
    rows_per_tile = tm // GRID_W
    parts = [jnp.broadcast_to(row_ref[pl.ds(i * rows_per_tile + k, 1), :], (GRID_W, HEAD_DIM)) + col_ref[...]
             for k in range(rows_per_tile)]
    return jnp.concatenate(parts, axis=0)


def _in_proj_kernel(x_ref, sc_ref, sh_ref, w_ref, cosr_ref, sar_ref, sbr_ref, cosc_ref, sac_ref, sbc_ref,
                    o_ref, h_ref):
    i = pl.program_id(0)
    tm = x_ref.shape[0]
    h_ref[...] = (x_ref[...] * (1.0 + sc_ref[...]) + sh_ref[...]).astype(BF16)
    cos_t = _rope_tile(cosr_ref, cosc_ref, i, tm)
    sa_t = _rope_tile(sar_ref, sac_ref, i, tm)
    sb_t = _rope_tile(sbr_ref, sbc_ref, i, tm)
    for j in range(IN_DIM // PROJ_TN):
        acc = _dot(h_ref[...], w_ref[:, j * PROJ_TN:(j + 1) * PROJ_TN])
        if j <= 1:
            scale = Q_SCALE if j == 0 else 1.0
            cos = cos_t * scale
            sa = sa_t * scale
            sb = sb_t * scale
            for k in range(PROJ_TN // HEAD_DIM):
                a = acc[:, k * HEAD_DIM:(k + 1) * HEAD_DIM]
                r = a * cos + pltpu.roll(a, 96, 1) * sa + pltpu.roll(a, 32, 1) * sb
                o_ref[:, j * PROJ_TN + k * HEAD_DIM:j * PROJ_TN + (k + 1) * HEAD_DIM] = r.astype(BF16)
        elif j == 3:
            o_ref[:, j * PROJ_TN:(j + 1) * PROJ_TN] = (acc * Q_SCALE).astype(BF16)
        else:
            o_ref[:, j * PROJ_TN:(j + 1) * PROJ_TN] = acc.astype(BF16)


def _in_proj(x, sc, sh, w, layer, cos_r, sa_r, sb_r, cos_c, sa_c, sb_c, tm):
    m = x.shape[0]
    nrow = cos_r.shape[0]
    tab_r = pl.BlockSpec((nrow, HEAD_DIM), lambda i: (0, 0))
    tab_c = pl.BlockSpec((GRID_W, HEAD_DIM), lambda i: (0, 0))
    return pl.pallas_call(
        _in_proj_kernel,
        grid=(m // tm,),
        in_specs=[
            pl.BlockSpec((tm, D_MODEL), lambda i: (i, 0)),
            pl.BlockSpec((1, D_MODEL), lambda i: (0, 0)),
            pl.BlockSpec((1, D_MODEL), lambda i: (0, 0)),
            pl.BlockSpec((None, D_MODEL, IN_DIM), lambda i: (layer, 0, 0), pipeline_mode=pl.Buffered(1)),
            tab_r, tab_r, tab_r, tab_c, tab_c, tab_c,
        ],
        out_specs=pl.BlockSpec((tm, IN_DIM), lambda i: (i, 0)),
        out_shape=jax.ShapeDtypeStruct((m, IN_DIM), BF16),
        scratch_shapes=[pltpu.VMEM((tm, D_MODEL), BF16)],
        compiler_params=_params(("parallel",)),
        name="in_proj",
    )(x, sc, sh, w, cos_r, sa_r, sb_r, cos_c, sa_c, sb_c)


UNROLL = 8


def _lambda_value(lv, lambda_init):
    a = jnp.sum(lv[0:1, :] * lv[1:2, :], axis=-1, keepdims=True)
    b = jnp.sum(lv[2:3, :] * lv[3:4, :], axis=-1, keepdims=True)
    return jnp.exp(a) - jnp.exp(b) + lambda_init


def _subln(o, g, lambda_init):
    ms = jnp.mean(o * o, axis=-1, keepdims=True)
    return o * lax.rsqrt(ms + LN_EPS) * g * (1.0 - lambda_init)


def _diff_attn_kernel(q_ref, k_ref, v_ref, kc_ref, vc_ref, lam_ref, g_ref, o_ref,
                      vt_ref, acc1_ref, acc2_ref, sa1_ref, sa2_ref, sb1_ref, sb2_ref, *, tk, lambda_init):
    nk = k_ref.shape[0] // tk
    accs = (acc1_ref, acc2_ref)
    buf_a = (sa1_ref, sa2_ref)
    buf_b = (sb1_ref, sb2_ref)

    @pl.when(pl.program_id(1) == 0)
    def _():
        def fill(i, carry):
            off = pl.multiple_of(i * tk, tk)
            vt_ref[i] = v_ref[pl.ds(off, tk), :].astype(F32).T.astype(BF16)
            return carry
        lax.fori_loop(0, nk, fill, 0)

    q_t = q_ref[...].astype(F32).T.astype(BF16)
    qs = (q_t[0:HEAD_DIM, :], q_t[HEAD_DIM:2 * HEAD_DIM, :])

    def scores_into(i, dst):
        off = pl.multiple_of(i * tk, tk)
        mx = []
        for c in range(2):
            s = _dot(k_ref[pl.ds(off, tk), c * HEAD_DIM:(c + 1) * HEAD_DIM], qs[c])
            dst[c][...] = s
            mx.append(jnp.max(s, axis=0, keepdims=True))
        return tuple(mx)

    def softmax_pv(i, src, mx, carry):
        vt = vt_ref[i]
        out = []
        for c in range(2):
            m, l = carry[2 * c], carry[2 * c + 1]
            mn = jnp.maximum(m, mx[c])
            a = jnp.exp2(m - mn)
            p = jnp.exp2(src[c][...] - mn)
            l = a * l + jnp.sum(p, axis=0, keepdims=True)
            accs[c][...] = a * accs[c][...] + _dot(vt, p.astype(BF16))
            out += [mn, l]
        return tuple(out)

    mx_a = scores_into(0, buf_a)

    vc_t = vc_ref[...].astype(F32).T.astype(BF16)
    carry = []
    for c in range(2):
        s = _dot(kc_ref[:, c * HEAD_DIM:(c + 1) * HEAD_DIM], qs[c])
        m = jnp.max(s, axis=0, keepdims=True)
        p = jnp.exp2(s - m)
        l = jnp.sum(p, axis=0, keepdims=True)
        accs[c][...] = _dot(vc_t, p.astype(BF16))
        carry += [m, l]

    bufs = (buf_a, buf_b)
    unroll = math.gcd(nk, UNROLL)
    assert unroll % 2 == 0

    def blocks(base, mx, carry, produce_last):
        for u in range(unroll):
            mx_next = None
            if produce_last or u < unroll - 1:
                mx_next = scores_into(base + u + 1, bufs[(u + 1) % 2])
            carry = softmax_pv(base + u, bufs[u % 2], mx, carry)
            mx = mx_next
        return mx, carry

    def body(j, state):
        return blocks(unroll * j, state[0], state[1], True)

    mx_a, carry = lax.fori_loop(0, nk // unroll - 1, body, (mx_a, tuple(carry)))
    _, (m1, l1, m2, l2) = blocks(nk - unroll, mx_a, carry, False)

    lam = _lambda_value(lam_ref[...], lambda_init)
    o_t = acc1_ref[...] / l1 - lam * (acc2_ref[...] / l2)
    ms = jnp.mean(o_t * o_t, axis=0, keepdims=True)
    o = (o_t * lax.rsqrt(ms + LN_EPS)).T
    o_ref[...] = (o * g_ref[...] * (1.0 - lambda_init)).astype(BF16)


def _diff_attn(qkv, qkvc, lam_vec, subln_g, lambda_init, tq, tk):
    l = qkv.shape[0]
    lc = qkvc.shape[0]
    kern = functools.partial(_diff_attn_kernel, tk=tk, lambda_init=lambda_init)
    return pl.pallas_call(
        kern,
        grid=(DA_HEADS, l // tq),
        in_specs=[
            pl.BlockSpec((tq, 256), lambda h, i: (i, QD_COL // 2 + h)),
            pl.BlockSpec((l, 256), lambda h, i: (0, KD_COL // 2 + h)),
            pl.BlockSpec((l, 256), lambda h, i: (0, VD_COL // 2 + h)),
            pl.BlockSpec((lc, 256), lambda h, i: (0, KD_COL // 2 + h)),
            pl.BlockSpec((lc, 256), lambda h, i: (0, VD_COL // 2 + h)),
            pl.BlockSpec((4, HEAD_DIM), lambda h, i: (0, 0)),
            pl.BlockSpec((1, 2 * HEAD_DIM), lambda h, i: (0, 0)),
        ],
        out_specs=pl.BlockSpec((tq, 256), lambda h, i: (i, h)),
        out_shape=jax.ShapeDtypeStruct((l, DA_WIDTH), BF16),
        scratch_shapes=[pltpu.VMEM((l // tk, 256, tk), BF16),
                        pltpu.VMEM((256, tq), F32), pltpu.VMEM((256, tq), F32)]
        + [pltpu.VMEM((tk, tq), F32)] * 4,
        compiler_params=_params(("parallel", "arbitrary")),
        name="diff_attn",
    )(qkv, qkv, qkv, qkvc, qkvc, lam_vec, subln_g)


NA_ROWS = 4
NA_TQ = NA_ROWS * GRID_W
NA_SLAB = NA_ROWS + NA_KH - 1
NA_SK = NA_SLAB * GRID_W


def _na_plan(l):
    rows = l // GRID_W
    kh = min(NA_KH, rows)
    assert kh == NA_KH and rows >= NA_SLAB and rows % NA_ROWS == 0
    nblk = rows // NA_ROWS
    starts, pats, geoms = [], [], []
    for b in range(nblk):
        r0 = b * NA_ROWS
        kstart = int(np.clip(r0 - kh // 2, 0, rows - NA_SLAB))
        rs = [int(np.clip(r0 + qr - kh // 2, 0, rows - kh)) for qr in range(NA_ROWS)]
        geom = (r0 - kstart, tuple(x - kstart for x in rs))
        assert all(0 <= x and x + kh <= NA_SLAB for x in geom[1])
        if geom not in geoms:
            geoms.append(geom)
        starts.append(kstart * GRID_W)
        pats.append(geoms.index(geom))
    return np.asarray(starts, np.int32), np.asarray(pats, np.int32), geoms


def _na_bias_kernel(t2_ref, o_ref, *, geoms):
    p = pl.program_id(1)
    lane = lax.broadcasted_iota(jnp.int32, (GRID_W, 2 * GRID_W), 1)
    for ps, (dq, rs) in enumerate(geoms):
        @pl.when(p == ps)
        def _(dq=dq, rs=rs):
            for qr in range(NA_ROWS):
                def inside(j, qr=qr):
                    return j < NA_SLAB and rs[qr] <= j < rs[qr] + NA_KH
                for j0 in range(0, NA_SLAB, 2):
                    left, right = inside(j0), inside(j0 + 1)
                    d0 = j0 - (dq + qr) + (NA_KH - 1)
                    if left or right:
                        blk = t2_ref[d0 + 1]
                        if not left:
                            blk = jnp.where(lane < GRID_W, MASK_VALUE, blk)
                        if not right:
                            blk = jnp.where(lane < GRID_W, blk, MASK_VALUE)
                    else:
                        blk = jnp.full((GRID_W, 2 * GRID_W), MASK_VALUE, F32)
                    width = min(2, NA_SLAB - j0) * GRID_W
                    o_ref[qr * GRID_W:(qr + 1) * GRID_W, j0 * GRID_W:j0 * GRID_W + width] = blk[:, :width]


def _na_bias_table(rpb, geoms):
    c = np.arange(GRID_W)
    dc = c[None, :] - c[:, None] + (NA_KW - 1)
    cs = np.clip(c - NA_KW // 2, 0, GRID_W - NA_KW)
    col_ok = (c[None, :] >= cs[:, None]) & (c[None, :] < cs[:, None] + NA_KW)
    e_col = np.zeros((2 * NA_KW - 1, GRID_W, GRID_W), np.float32)
    for ci in range(GRID_W):
        for ki in range(GRID_W):
            if col_ok[ci, ki]:
                e_col[dc[ci, ki], ci, ki] = 1.0
    t = jnp.einsum('hrd,dck->hrck', rpb.astype(F32), jnp.asarray(e_col), precision=lax.Precision.HIGHEST)
    t = jnp.where(jnp.asarray(col_ok)[None, None], t * LOG2E, MASK_VALUE)
    tp = jnp.pad(t, ((0, 0), (1, 1), (0, 0), (0, 0)))
    t2 = jnp.concatenate([tp[:, :-1], tp[:, 1:]], axis=-1)
    npat = len(geoms)
    return pl.pallas_call(
        functools.partial(_na_bias_kernel, geoms=tuple(geoms)),
        grid=(NA_HEADS, npat),
        in_specs=[pl.BlockSpec((None, 2 * NA_KH, GRID_W, 2 * GRID_W), lambda h, p: (h, 0, 0, 0))],
        out_specs=pl.BlockSpec((None, None, NA_TQ, NA_SK), lambda h, p: (p, h, 0, 0)),
        out_shape=jax.ShapeDtypeStruct((npat, NA_HEADS, NA_TQ, NA_SK), F32),
        compiler_params=_params(("parallel", "arbitrary")),
        name="na_bias",
    )(t2)


NA_HPS = 4


def _na_attn_kernel(start_ref, pat_ref, q_ref, k_ref, v_ref, kc_ref, vc_ref, bias_ref, o_ref):
    del pat_ref
    i = pl.program_id(1)
    off = pl.multiple_of(start_ref[i], GRID_W)
    for h in range(NA_HPS):
        cols = slice(h * HEAD_DIM, (h + 1) * HEAD_DIM)
        q = q_ref[:, cols]
        s_loc = _dot_nt(q, k_ref[pl.ds(off, NA_SK), cols]) + bias_ref[h]
        s_ctx = _dot_nt(q, kc_ref[:, cols])
        m = jnp.maximum(jnp.max(s_loc, axis=-1, keepdims=True), jnp.max(s_ctx, axis=-1, keepdims=True))
        p_loc = jnp.exp2(s_loc - m)
        p_ctx = jnp.exp2(s_ctx - m)
        l = jnp.sum(p_loc, axis=-1, keepdims=True) + jnp.sum(p_ctx, axis=-1, keepdims=True)
        o = (_dot(p_loc.astype(BF16), v_ref[pl.ds(off, NA_SK), cols])
             + _dot(p_ctx.astype(BF16), vc_ref[:, cols]))
        o_ref[:, cols] = (o / l).astype(BF16)


def _na_attn(qkv, qkvc, bias, starts, pats):
    l = qkv.shape[0]
    lc = qkvc.shape[0]
    w = NA_HPS * HEAD_DIM
    once = pl.Buffered(1)
    grid_spec = pltpu.PrefetchScalarGridSpec(
        num_scalar_prefetch=2,
        grid=(NA_HEADS // NA_HPS, l // NA_TQ),
        in_specs=[
            pl.BlockSpec((NA_TQ, w), lambda h, i, st, pt: (i, QN_COL // NA_HPS + h)),
            pl.BlockSpec((l, w), lambda h, i, st, pt: (0, KN_COL // NA_HPS + h), pipeline_mode=once),
            pl.BlockSpec((l, w), lambda h, i, st, pt: (0, VN_COL // NA_HPS + h), pipeline_mode=once),
            pl.BlockSpec((lc, w), lambda h, i, st, pt: (0, KN_COL // NA_HPS + h)),
            pl.BlockSpec((lc, w), lambda h, i, st, pt: (0, VN_COL // NA_HPS + h)),
            pl.BlockSpec((None, NA_HPS, NA_TQ, NA_SK), lambda h, i, st, pt: (pt[i], h, 0, 0)),
        ],
        out_specs=pl.BlockSpec((NA_TQ, w), lambda h, i, st, pt: (i, h)),
    )
    return pl.pallas_call(
        _na_attn_kernel,
        grid_spec=grid_spec,
        out_shape=jax.ShapeDtypeStruct((l, NA_WIDTH), BF16),
        compiler_params=_params(("parallel", "arbitrary")),
        name="na_attn",
    )(starts, pats, qkv, qkv, qkv, qkvc, qkvc, bias)


def _softmax_pv(q, k, v):
    s = _dot_nt(q, k)
    m = jnp.max(s, axis=-1, keepdims=True)
    p = jnp.exp2(s - m)
    l = jnp.sum(p, axis=-1, keepdims=True)
    return _dot(p.astype(BF16), v) / l


def _ctx_attn_kernel(x_ref, lam_ref, g_ref, od_ref, on_ref, *, lambda_init):
    lam = _lambda_value(lam_ref[...], lambda_init)
    g = g_ref[...]

    def col(base, h, width=HEAD_DIM):
        lo = (base + h) * HEAD_DIM
        return x_ref[:, lo:lo + width]

    for h in range(DA_HEADS):
        v = col(VD_COL, 2 * h, 2 * HEAD_DIM)
        o1 = _softmax_pv(col(QD_COL, 2 * h), col(KD_COL, 2 * h), v)
        o2 = _softmax_pv(col(QD_COL, 2 * h + 1), col(KD_COL, 2 * h + 1), v)
        od_ref[:, h * 256:(h + 1) * 256] = _subln(o1 - lam * o2, g, lambda_init).astype(BF16)
    for h in range(NA_HEADS):
        o = _softmax_pv(col(QN_COL, h), col(KN_COL, h), col(VN_COL, h))
        on_ref[:, h * HEAD_DIM:(h + 1) * HEAD_DIM] = o.astype(BF16)


def _ctx_attn(qkvc, lam_vec, subln_g, lambda_init):
    lc = qkvc.shape[0]
    kern = functools.partial(_ctx_attn_kernel, lambda_init=lambda_init)
    return pl.pallas_call(
        kern,
        grid=(1,),
        in_specs=[
            pl.BlockSpec((lc, IN_DIM), lambda i: (0, 0)),
            pl.BlockSpec((4, HEAD_DIM), lambda i: (0, 0)),
            pl.BlockSpec((1, 2 * HEAD_DIM), lambda i: (0, 0)),
        ],
        out_specs=[pl.BlockSpec((lc, DA_WIDTH), lambda i: (0, 0)),
                   pl.BlockSpec((lc, NA_WIDTH), lambda i: (0, 0))],
        out_shape=[jax.ShapeDtypeStruct((lc, DA_WIDTH), BF16),
                   jax.ShapeDtypeStruct((lc, NA_WIDTH), BF16)],
        compiler_params=_params(("arbitrary",)),
        name="ctx_attn",
    )(qkvc, lam_vec, subln_g)


OUT_SUB = 256


def _out_proj_kernel(od_ref, on_ref, w_ref, x_ref, gate_ref, g_ref, b_ref, o_ref):
    tm = x_ref.shape[0]
    for r0 in range(0, tm, OUT_SUB):
        rows = slice(r0, r0 + OUT_SUB)
        y = _dot(od_ref[rows, :], w_ref[0:DA_WIDTH, :]) + _dot(on_ref[rows, :], w_ref[DA_WIDTH:, :])
        z = ALPHA * x_ref[rows, :] + gate_ref[...] * y
        o_ref[rows, :] = _layernorm_rows(z, g_ref[...], b_ref[...])


def _out_proj_ln(od, on, w, layer, x, gate, ln_g, ln_b, tm):
    m = x.shape[0]
    vec = pl.BlockSpec((1, D_MODEL), lambda i: (0, 0))
    return pl.pallas_call(
        _out_proj_kernel,
        grid=(m // tm,),
        in_specs=[
            pl.BlockSpec((tm, DA_WIDTH), lambda i: (i, 0)),
            pl.BlockSpec((tm, NA_WIDTH), lambda i: (i, 0)),
            pl.BlockSpec((None, DA_WIDTH + NA_WIDTH, D_MODEL), lambda i: (layer, 0, 0),
                         pipeline_mode=pl.Buffered(1)),
            pl.BlockSpec((tm, D_MODEL), lambda i: (i, 0)),
            vec, vec, vec,
        ],
        out_specs=pl.BlockSpec((tm, D_MODEL), lambda i: (i, 0)),
        out_shape=jax.ShapeDtypeStruct((m, D_MODEL), F32),
        compiler_params=_params(("parallel",)),
        name="out_proj_ln",
    )(od, on, w, x, gate, ln_g, ln_b)


FFN_TF = 512
HALO = 16


def _ffn_kernel(x_ref, xp_ref, xn_ref, sc_ref, sh_ref, gate_ref, wg_ref, wu_ref, cw_ref, cb_ref, wd_ref,
                g_ref, b_ref, o_ref, h_ref, acc_ref):
    i = pl.program_id(0)
    c = pl.program_id(1)
    tm = x_ref.shape[0]
    n = tm + 2 * HALO

    @pl.when(c == 0)
    def _():
        sc = 1.0 + sc_ref[...]
        sh = sh_ref[...]
        before = jnp.where(i == 0, 0.0, xp_ref[...] * sc + sh)
        after = jnp.where(i == pl.num_programs(0) - 1, 0.0, xn_ref[...] * sc + sh)
        h_ref[0:HALO, :] = before.astype(BF16)
        h_ref[HALO:HALO + tm, :] = (x_ref[...] * sc + sh).astype(BF16)
        h_ref[HALO + tm:n, :] = after.astype(BF16)
        acc_ref[...] = jnp.zeros_like(acc_ref)

    g_ext = _dot(h_ref[...], wg_ref[...])
    u = _dot(h_ref[HALO:HALO + tm, :], wu_ref[...])
    g = g_ext[HALO:HALO + tm, :]
    g_prev = pltpu.roll(g_ext, 1, 0)[HALO:HALO + tm, :]
    g_next = pltpu.roll(g_ext, n - 1, 0)[HALO:HALO + tm, :]
    gc = g_prev * cw_ref[0:1, :] + g * cw_ref[1:2, :] + g_next * cw_ref[2:3, :] + cb_ref[...]
    act = gc / (1.0 + jnp.exp(-gc)) * u
    acc_ref[...] += _dot(act.astype(BF16), wd_ref[...])

    @pl.when(c == pl.num_programs(1) - 1)
    def _():
        z = ALPHA * x_ref[...] + gate_ref[...] * acc_ref[...]
        o_ref[...] = _layernorm_rows(z, g_ref[...], b_ref[...])


def _ffn_ln(x, sc, sh, gate, w_up, conv_w, conv_b, w_down, layer, ln_g, ln_b, tm):
    m = x.shape[0]
    nf = D_FF // FFN_TF
    tb = tm // HALO
    last_blk = m // HALO - 1
    vec = pl.BlockSpec((1, D_MODEL), lambda i, c: (0, 0))
    return pl.pallas_call(
        _ffn_kernel,
        grid=(m // tm, nf),
        in_specs=[
            pl.BlockSpec((tm, D_MODEL), lambda i, c: (i, 0)),
            pl.BlockSpec((HALO, D_MODEL), lambda i, c: (jnp.maximum(i * tb - 1, 0), 0)),
            pl.BlockSpec((HALO, D_MODEL), lambda i, c: (jnp.minimum((i + 1) * tb, last_blk), 0)),
            vec, vec, vec,
            pl.BlockSpec((None, D_MODEL, FFN_TF), lambda i, c: (layer, 0, c)),
            pl.BlockSpec((None, D_MODEL, FFN_TF), lambda i, c: (layer, 0, nf + c)),
            pl.BlockSpec((3, FFN_TF), lambda i, c: (0, c)),
            pl.BlockSpec((1, FFN_TF), lambda i, c: (0, c)),
            pl.BlockSpec((None, FFN_TF, D_MODEL), lambda i, c: (layer, c, 0)),
            vec, vec,
        ],
        out_specs=pl.BlockSpec((tm, D_MODEL), lambda i, c: (i, 0)),
        out_shape=jax.ShapeDtypeStruct((m, D_MODEL), F32),
        scratch_shapes=[pltpu.VMEM((tm + 2 * HALO, D_MODEL), BF16),
                        pltpu.VMEM((tm, D_MODEL), F32)],
        compiler_params=_params(("parallel", "arbitrary")),
        name="ffn_ln",
    )(x, x, x, sc, sh, gate, w_up, w_up, conv_w, conv_b, w_down, ln_g, ln_b)


def _rope_tables(l):
    half = HEAD_DIM // 2
    inv_freq = 1.0 / (ROPE_THETA ** (jnp.arange(0, half, 2, dtype=F32) / half))
    zeros = jnp.zeros((1, half), F32)

    def tables(pos, row_part):
        ang = pos.astype(F32)[:, None] * inv_freq[None, :]
        ang = jnp.concatenate([ang, ang], axis=-1)
        cos, sin = jnp.cos(ang), jnp.sin(ang)
        first = jnp.asarray(np.arange(half) < half // 2)[None, :]
        sin_a = jnp.where(first, -sin, 0.0)
        sin_b = jnp.where(first, 0.0, sin)
        pad = jnp.broadcast_to(zeros, cos.shape)
        order = (lambda t: jnp.concatenate([t, pad], axis=-1)) if row_part else \
                (lambda t: jnp.concatenate([pad, t], axis=-1))
        return order(cos), order(sin_a), order(sin_b)

    rows = tables(jnp.arange(l // GRID_W, dtype=jnp.int32), True)
    cols = tables(jnp.arange(GRID_W, dtype=jnp.int32), False)
    return rows + cols


def _identity_rope_tables(lc):
    nrow = lc // GRID_W
    lanes = jnp.asarray(np.arange(HEAD_DIM) < HEAD_DIM // 2, F32)[None, :]
    zr = jnp.zeros((nrow, HEAD_DIM), F32)
    zc = jnp.zeros((GRID_W, HEAD_DIM), F32)
    return (jnp.broadcast_to(lanes, (nrow, HEAD_DIM)), zr, zr,
            jnp.broadcast_to(1.0 - lanes, (GRID_W, HEAD_DIM)), zc, zc)


def kernel(x, c, ctx, c_ctx, w_ada, b_ada, w_in, da_lambda, da_subln, na_rpb, w_o,
           ln1_g, ln1_b, w_up, conv_w, conv_b, w_down, ln2_g, ln2_b):
    assert x.shape[0] == 1 and ctx.shape[0] == 1
    l = x.shape[1]
    lc = ctx.shape[1]
    xs = x[0]
    xc = ctx[0]

    cond = jnp.broadcast_to(jnp.stack([c[0], c_ctx])[:, :, None], (2, D_MODEL, 128))
    mods = _ada_mod(cond, w_ada, b_ada)

    rope = _rope_tables(l)
    rope_ctx = _identity_rope_tables(lc)
    w_in_b = w_in.astype(BF16)
    w_o_b = w_o.astype(BF16)
    w_up_b = w_up.astype(BF16)
    w_down_b = w_down.astype(BF16)
    starts, pats, geoms = _na_plan(l)
    starts = jnp.asarray(starts)
    pats = jnp.asarray(pats)

    for layer in range(DEPTH):
        last = layer == DEPTH - 1
        lambda_init = 0.8 - 0.6 * math.exp(-0.3 * layer)
        sh_a, sc_a, g_a, sh_m, sc_m, g_m = jnp.split(mods[layer, 0:1], N_MOD, axis=-1)
        shc_a, scc_a, gc_a, shc_m, scc_m, gc_m = jnp.split(mods[layer, 1:2], N_MOD, axis=-1)
        lam_vec = da_lambda[layer]
        subln_g = da_subln[layer].reshape(1, 2 * HEAD_DIM)
        ln1 = (ln1_g[layer].reshape(1, D_MODEL), ln1_b[layer].reshape(1, D_MODEL))
        ln2 = (ln2_g[layer].reshape(1, D_MODEL), ln2_b[layer].reshape(1, D_MODEL))
        cw = conv_w[layer]
        cb = conv_b[layer].reshape(1, D_FF)

        qkv = _in_proj(xs, sc_a, sh_a, w_in_b, layer, *rope, tm=256)
        qkvc = _in_proj(xc, scc_a, shc_a, w_in_b, layer, *rope_ctx, tm=lc)

        od = _diff_attn(qkv, qkvc, lam_vec, subln_g, lambda_init, tq=512, tk=512)
        bias = _na_bias_table(na_rpb[layer], geoms)
        on = _na_attn(qkv, qkvc, bias, starts, pats)

        xs = _out_proj_ln(od, on, w_o_b, layer, xs, g_a, *ln1, tm=512)
        xs = _ffn_ln(xs, sc_m, sh_m, g_m, w_up_b, cw, cb, w_down_b, layer, *ln2, tm=512)

        if not last:
            odc, onc = _ctx_attn(qkvc, lam_vec, subln_g, lambda_init)
            xc = _out_proj_ln(odc, onc, w_o_b, layer, xc, gc_a, *ln1, tm=lc)
            xc = _ffn_ln(xc, scc_m, shc_m, gc_m, w_up_b, cw, cb, w_down_b, layer, *ln2, tm=lc)

    return xs[None]
```

```python
import functools
import math

import numpy as np
import jax
import jax.numpy as jnp
from jax import lax
from jax.experimental import pallas as pl
from jax.experimental.pallas import tpu as pltpu

D_MODEL = 2048
DEPTH = 2
GRID_W = 64
HEAD_DIM = 128
DA_HEADS = 4
NA_HEADS = 8
DA_WIDTH = DA_HEADS * 2 * HEAD_DIM
NA_WIDTH = NA_HEADS * HEAD_DIM
IN_DIM = 3 * DA_WIDTH + 3 * NA_WIDTH
NA_KH = 8
NA_KW = 16
D_FF = 5632
ROPE_THETA = 10000.0
LN_EPS = 1e-5
N_MOD = 6
ALPHA = (2.0 * DEPTH) ** 0.25
LOG2E = math.log2(math.e)
Q_SCALE = (HEAD_DIM ** -0.5) * LOG2E
MASK_VALUE = -1e30

F32 = jnp.float32
BF16 = jnp.bfloat16

VMEM_LIMIT = 56 * 1024 * 1024

QD_COL, KD_COL, VD_COL = 0, DA_WIDTH // 128, 2 * DA_WIDTH // 128
QN_COL = 3 * DA_WIDTH // 128
KN_COL = QN_COL + NA_WIDTH // 128
VN_COL = KN_COL + NA_WIDTH // 128


def _params(semantics):
    return pltpu.CompilerParams(dimension_semantics=semantics, vmem_limit_bytes=VMEM_LIMIT)


def _dot(a, b):
    return jnp.dot(a, b, preferred_element_type=F32)


def _dot_nt(a, b):
    return lax.dot_general(a, b, (((1,), (1,)), ((), ())), preferred_element_type=F32)


def _layernorm_rows(z, g, b):
    mu = jnp.mean(z, axis=-1, keepdims=True)
    zc = z - mu
    var = jnp.mean(zc * zc, axis=-1, keepdims=True)
    return zc * lax.rsqrt(var + LN_EPS) * g + b


ADA_TN = 1024
ADA_KC = 256


def _ada_kernel(c_ref, w_ref, b_ref, o_ref):
    reps = w_ref.shape[1] // 128
    for r in range(c_ref.shape[0]):
        acc = b_ref[...]
        for k0 in range(0, D_MODEL, ADA_KC):
            a = c_ref[r, k0:k0 + ADA_KC, :]
            silu = a / (1.0 + jnp.exp(-a))
            sw = jnp.concatenate([silu] * reps, axis=1)
            acc = acc + jnp.sum(w_ref[k0:k0 + ADA_KC, :] * sw, axis=0, keepdims=True)
        o_ref[r:r + 1, :] = acc


def _ada_mod(cond, w_ada, b_ada):
    nmod = N_MOD * D_MODEL
    nrow = cond.shape[0]
    return pl.pallas_call(
        _ada_kernel,
        grid=(DEPTH, nmod // ADA_TN),
        in_specs=[
            pl.BlockSpec((nrow, D_MODEL, 128), lambda l, j: (0, 0, 0)),
            pl.BlockSpec((None, D_MODEL, ADA_TN), lambda l, j: (l, 0, j)),
            pl.BlockSpec((None, 1, ADA_TN), lambda l, j: (l, 0, j)),
        ],
        out_specs=pl.BlockSpec((None, nrow, ADA_TN), lambda l, j: (l, 0, j)),
        out_shape=jax.ShapeDtypeStruct((DEPTH, nrow, nmod), F32),
        compiler_params=_params(("parallel", "parallel")),
        name="ada_mod",
    )(cond, w_ada, b_ada.reshape(DEPTH, 1, nmod))


PROJ_TN = 1024


def _rope_tile(row_ref, col_ref, i, tm):
    rows_per_tile = tm // GRID_W
    parts = [jnp.broadcast_to(row_ref[pl.ds(i * rows_per_tile + k, 1), :], (GRID_W, HEAD_DIM)) + col_ref[...]
             for k in range(rows_per_tile)]
    return jnp.concatenate(parts, axis=0)


def _in_proj_kernel(x_ref, sc_ref, sh_ref, w_ref, cosr_ref, sar_ref, sbr_ref, cosc_ref, sac_ref, sbc_ref,
                    o_ref, h_ref):
    i = pl.program_id(0)
    tm = x_ref.shape[0]
    h_ref[...] = (x_ref[...] * (1.0 + sc_ref[...]) + sh_ref[...]).astype(BF16)
    cos_t = _rope_tile(cosr_ref, cosc_ref, i, tm)
    sa_t = _rope_tile(sar_ref, sac_ref, i, tm)
    sb_t = _rope_tile(sbr_ref, sbc_ref, i, tm)
    for j in range(IN_DIM // PROJ_TN):
        acc = _dot(h_ref[...], w_ref[:, j * PROJ_TN:(j + 1) * PROJ_TN])
        if j <= 1:
            scale = Q_SCALE if j == 0 else 1.0
            cos = cos_t * scale
            sa = sa_t * scale
            sb = sb_t * scale
            for k in range(PROJ_TN // HEAD_DIM):
                a = acc[:, k * HEAD_DIM:(k + 1) * HEAD_DIM]
                r = a * cos + pltpu.roll(a, 96, 1) * sa + pltpu.roll(a, 32, 1) * sb
                o_ref[:, j * PROJ_TN + k * HEAD_DIM:j * PROJ_TN + (k + 1) * HEAD_DIM] = r.astype(BF16)
        elif j == 3:
            o_ref[:, j * PROJ_TN:(j + 1) * PROJ_TN] = (acc * Q_SCALE).astype(BF16)
        else:
            o_ref[:, j * PROJ_TN:(j + 1) * PROJ_TN] = acc.astype(BF16)


def _in_proj(x, sc, sh, w, layer, cos_r, sa_r, sb_r, cos_c, sa_c, sb_c, tm):
    m = x.shape[0]
    nrow = cos_r.shape[0]
    tab_r = pl.BlockSpec((nrow, HEAD_DIM), lambda i: (0, 0))
    tab_c = pl.BlockSpec((GRID_W, HEAD_DIM), lambda i: (0, 0))
    return pl.pallas_call(
        _in_proj_kernel,
        grid=(m // tm,),
        in_specs=[
            pl.BlockSpec((tm, D_MODEL), lambda i: (i, 0)),
            pl.BlockSpec((1, D_MODEL), lambda i: (0, 0)),
            pl.BlockSpec((1, D_MODEL), lambda i: (0, 0)),
            pl.BlockSpec((None, D_MODEL, IN_DIM), lambda i: (layer, 0, 0), pipeline_mode=pl.Buffered(1)),
            tab_r, tab_r, tab_r, tab_c, tab_c, tab_c,
        ],
        out_specs=pl.BlockSpec((tm, IN_DIM), lambda i: (i, 0)),
        out_shape=jax.ShapeDtypeStruct((m, IN_DIM), BF16),
        scratch_shapes=[pltpu.VMEM((tm, D_MODEL), BF16)],
        compiler_params=_params(("parallel",)),
        name="in_proj",
    )(x, sc, sh, w, cos_r, sa_r, sb_r, cos_c, sa_c, sb_c)


UNROLL = 8


def _lambda_value(lv, lambda_init):
    a = jnp.sum(lv[0:1, :] * lv[1:2, :], axis=-1, keepdims=True)
    b = jnp.sum(lv[2:3, :] * lv[3:4, :], axis=-1, keepdims=True)
    return jnp.exp(a) - jnp.exp(b) + lambda_init


def _subln(o, g, lambda_init):
    ms = jnp.mean(o * o, axis=-1, keepdims=True)
    return o * lax.rsqrt(ms + LN_EPS) * g * (1.0 - lambda_init)


def _diff_attn_kernel(q_ref, qn_ref, k_ref, v_ref, kc_ref, vc_ref, lam_ref, g_ref, o_ref,
                      vt_ref, qt_ref, mx_ref, acc1_ref, acc2_ref, sa1_ref, sa2_ref, sb1_ref, sb2_ref,
                      *, tk, lambda_init):
    nk = k_ref.shape[0] // tk
    accs = (acc1_ref, acc2_ref)
    bufs = ((sa1_ref, sa2_ref), (sb1_ref, sb2_ref))
    first = pl.program_id(1) == 0
    unroll = math.gcd(nk, UNROLL)
    assert unroll % 2 == 0
    trips = nk // unroll

    def scores_into(i, dst, slot):
        off = pl.multiple_of(i * tk, tk)
        mx = []
        for c in range(2):
            cols = slice(c * HEAD_DIM, (c + 1) * HEAD_DIM)
            s = _dot(k_ref[pl.ds(off, tk), cols], qt_ref[slot, cols, :])
            dst[c][...] = s
            mx.append(jnp.max(s, axis=0, keepdims=True))
        return tuple(mx)

    @pl.when(first)
    def _():
        def fill(i, carry):
            off = pl.multiple_of(i * tk, tk)
            vt_ref[i] = v_ref[pl.ds(off, tk), :].astype(F32).T.astype(BF16)
            return carry
        lax.fori_loop(0, nk, fill, 0)
        qt_ref[0] = q_ref[...].astype(F32).T.astype(BF16)
        mx0 = scores_into(0, bufs[0], 0)
        mx_ref[0:1, :] = mx0[0]
        mx_ref[1:2, :] = mx0[1]

    @pl.when(jnp.logical_not(first))
    def _():
        qt_ref[0] = qt_ref[1]

    def softmax_pv(i, src, mx, carry):
        vt = vt_ref[i]
        out = []
        for c in range(2):
            if carry is None:
                mn = mx[c]
                p = jnp.exp2(src[c][...] - mn)
                l = jnp.sum(p, axis=0, keepdims=True)
                accs[c][...] = _dot(vt, p.astype(BF16))
            else:
                m, l = carry[2 * c], carry[2 * c + 1]
                mn = jnp.maximum(m, mx[c])
                a = jnp.exp2(m - mn)
                p = jnp.exp2(src[c][...] - mn)
                l = a * l + jnp.sum(p, axis=0, keepdims=True)
                accs[c][...] = a * accs[c][...] + _dot(vt, p.astype(BF16))
            out += [mn, l]
        return tuple(out)

    def context_step(carry):
        vc_t = vc_ref[...].astype(F32).T.astype(BF16)
        out = []
        for c in range(2):
            cols = slice(c * HEAD_DIM, (c + 1) * HEAD_DIM)
            m, l = carry[2 * c], carry[2 * c + 1]
            s = _dot(kc_ref[:, cols], qt_ref[0, cols, :])
            mn = jnp.maximum(m, jnp.max(s, axis=0, keepdims=True))
            a = jnp.exp2(m - mn)
            p = jnp.exp2(s - mn)
            l = a * l + jnp.sum(p, axis=0, keepdims=True)
            accs[c][...] = a * accs[c][...] + _dot(vc_t, p.astype(BF16))
            out += [mn, l]
        return tuple(out)

    def blocks(trip, mx, carry):
        base = trip * unroll
        for u in range(unroll):
            if u == unroll - 1:
                wrap = trip == trips - 1
                nxt = jnp.where(wrap, 0, base + unroll)
                mx_next = scores_into(nxt, bufs[(u + 1) % 2], jnp.where(wrap, 1, 0))
            else:
                mx_next = scores_into(base + u + 1, bufs[(u + 1) % 2], 0)
            start = carry is None
            carry = softmax_pv(base + u, bufs[u % 2], mx, carry)
            if start:
                carry = context_step(carry)
                qt_ref[1] = qn_ref[...].astype(F32).T.astype(BF16)
            mx = mx_next
        return mx, carry

    state = blocks(0, (mx_ref[0:1, :], mx_ref[1:2, :]), None)

    mx_n, (m1, l1, m2, l2) = lax.fori_loop(1, trips, lambda j, st: blocks(j, st[0], st[1]), state)
    mx_ref[0:1, :] = mx_n[0]
    mx_ref[1:2, :] = mx_n[1]

    lam = _lambda_value(lam_ref[...], lambda_init)
    o_t = acc1_ref[...] / l1 - lam * (acc2_ref[...] / l2)
    ms = jnp.mean(o_t * o_t, axis=0, keepdims=True)
    o = (o_t * lax.rsqrt(ms + LN_EPS)).T
    o_ref[...] = (o * g_ref[...] * (1.0 - lambda_init)).astype(BF16)


def _diff_attn(qkv, qkvc, lam_vec, subln_g, lambda_init, tq, tk):
    l = qkv.shape[0]
    lc = qkvc.shape[0]
    kern = functools.partial(_diff_attn_kernel, tk=tk, lambda_init=lambda_init)
    return pl.pallas_call(
        kern,
        grid=(DA_HEADS, l // tq),
        in_specs=[
            pl.BlockSpec((tq, 256), lambda h, i: (i, QD_COL // 2 + h)),
            pl.BlockSpec((tq, 256), lambda h, i: (jnp.minimum(i + 1, l // tq - 1), QD_COL // 2 + h)),
            pl.BlockSpec((l, 256), lambda h, i: (0, KD_COL // 2 + h)),
            pl.BlockSpec((l, 256), lambda h, i: (0, VD_COL // 2 + h)),
            pl.BlockSpec((lc, 256), lambda h, i: (0, KD_COL // 2 + h)),
            pl.BlockSpec((lc, 256), lambda h, i: (0, VD_COL // 2 + h)),
            pl.BlockSpec((4, HEAD_DIM), lambda h, i: (0, 0)),
            pl.BlockSpec((1, 2 * HEAD_DIM), lambda h, i: (0, 0)),
        ],
        out_specs=pl.BlockSpec((tq, 256), lambda h, i: (i, h)),
        out_shape=jax.ShapeDtypeStruct((l, DA_WIDTH), BF16),
        scratch_shapes=[pltpu.VMEM((l // tk, 256, tk), BF16),
                        pltpu.VMEM((2, 256, tq), BF16), pltpu.VMEM((8, tq), F32),
                        pltpu.VMEM((256, tq), F32), pltpu.VMEM((256, tq), F32)]
        + [pltpu.VMEM((tk, tq), F32)] * 4,
        compiler_params=_params(("parallel", "arbitrary")),
        name="diff_attn",
    )(qkv, qkv, qkv, qkv, qkvc, qkvc, lam_vec, subln_g)


NA_ROWS = 4
NA_TQ = NA_ROWS * GRID_W
NA_SLAB = NA_ROWS + NA_KH - 1
NA_SK = NA_SLAB * GRID_W


def _na_plan(l):
    rows = l // GRID_W
    kh = min(NA_KH, rows)
    assert kh == NA_KH and rows >= NA_SLAB and rows % NA_ROWS == 0
    nblk = rows // NA_ROWS
    starts, pats, geoms = [], [], []
    for b in range(nblk):
        r0 = b * NA_ROWS
        kstart = int(np.clip(r0 - kh // 2, 0, rows - NA_SLAB))
        rs = [int(np.clip(r0 + qr - kh // 2, 0, rows - kh)) for qr in range(NA_ROWS)]
        geom = (r0 - kstart, tuple(x - kstart for x in rs))
        assert all(0 <= x and x + kh <= NA_SLAB for x in geom[1])
        if geom not in geoms:
            geoms.append(geom)
        starts.append(kstart * GRID_W)
        pats.append(geoms.index(geom))
    return np.asarray(starts, np.int32), np.asarray(pats, np.int32), geoms


def _na_bias_kernel(t2_ref, o_ref, *, geoms):
    p = pl.program_id(1)
    lane = lax.broadcasted_iota(jnp.int32, (GRID_W, 2 * GRID_W), 1)
    for ps, (dq, rs) in enumerate(geoms):
        @pl.when(p == ps)
        def _(dq=dq, rs=rs):
            for qr in range(NA_ROWS):
                def inside(j, qr=qr):
                    return j < NA_SLAB and rs[qr] <= j < rs[qr] + NA_KH
                for j0 in range(0, NA_SLAB, 2):
                    left, right = inside(j0), inside(j0 + 1)
                    d0 = j0 - (dq + qr) + (NA_KH - 1)
                    if left or right:
                        blk = t2_ref[d0 + 1]
                        if not left:
                            blk = jnp.where(lane < GRID_W, MASK_VALUE, blk)
                        if not right:
                            blk = jnp.where(lane < GRID_W, blk, MASK_VALUE)
                    else:
                        blk = jnp.full((GRID_W, 2 * GRID_W), MASK_VALUE, F32)
                    width = min(2, NA_SLAB - j0) * GRID_W
                    o_ref[qr * GRID_W:(qr + 1) * GRID_W, j0 * GRID_W:j0 * GRID_W + width] = blk[:, :width]


def _na_bias_table(rpb, geoms):
    c = np.arange(GRID_W)
    dc = c[None, :] - c[:, None] + (NA_KW - 1)
    cs = np.clip(c - NA_KW // 2, 0, GRID_W - NA_KW)
    col_ok = (c[None, :] >= cs[:, None]) & (c[None, :] < cs[:, None] + NA_KW)
    e_col = np.zeros((2 * NA_KW - 1, GRID_W, GRID_W), np.float32)
    for ci in range(GRID_W):
        for ki in range(GRID_W):
            if col_ok[ci, ki]:
                e_col[dc[ci, ki], ci, ki] = 1.0
    t = jnp.einsum('hrd,dck->hrck', rpb.astype(F32), jnp.asarray(e_col), precision=lax.Precision.HIGHEST)
    t = jnp.where(jnp.asarray(col_ok)[None, None], t * LOG2E, MASK_VALUE)
    tp = jnp.pad(t, ((0, 0), (1, 1), (0, 0), (0, 0)))
    t2 = jnp.concatenate([tp[:, :-1], tp[:, 1:]], axis=-1)
    npat = len(geoms)
    return pl.pallas_call(
        functools.partial(_na_bias_kernel, geoms=tuple(geoms)),
        grid=(NA_HEADS, npat),
        in_specs=[pl.BlockSpec((None, 2 * NA_KH, GRID_W, 2 * GRID_W), lambda h, p: (h, 0, 0, 0))],
        out_specs=pl.BlockSpec((None, None, NA_TQ, NA_SK), lambda h, p: (p, h, 0, 0)),
        out_shape=jax.ShapeDtypeStruct((npat, NA_HEADS, NA_TQ, NA_SK), F32),
        compiler_params=_params(("parallel", "arbitrary")),
        name="na_bias",
    )(t2)


NA_HPS = 4


def _na_attn_kernel(start_ref, pat_ref, q_ref, k_ref, v_ref, kc_ref, vc_ref, bias_ref, o_ref):
    del pat_ref
    i = pl.program_id(1)
    off = pl.multiple_of(start_ref[i], GRID_W)
    for h in range(NA_HPS):
        cols = slice(h * HEAD_DIM, (h + 1) * HEAD_DIM)
        q = q_ref[:, cols]
        s_loc = _dot_nt(q, k_ref[pl.ds(off, NA_SK), cols]) + bias_ref[h]
        s_ctx = _dot_nt(q, kc_ref[:, cols])
        m = jnp.maximum(jnp.max(s_loc, axis=-1, keepdims=True), jnp.max(s_ctx, axis=-1, keepdims=True))
        p_loc = jnp.exp2(s_loc - m)
        p_ctx = jnp.exp2(s_ctx - m)
        l = jnp.sum(p_loc, axis=-1, keepdims=True) + jnp.sum(p_ctx, axis=-1, keepdims=True)
        o = (_dot(p_loc.astype(BF16), v_ref[pl.ds(off, NA_SK), cols])
             + _dot(p_ctx.astype(BF16), vc_ref[:, cols]))
        o_ref[:, cols] = (o / l).astype(BF16)


def _na_attn(qkv, qkvc, bias, starts, pats):
    l = qkv.shape[0]
    lc = qkvc.shape[0]
    w = NA_HPS * HEAD_DIM
    once = pl.Buffered(1)
    grid_spec = pltpu.PrefetchScalarGridSpec(
        num_scalar_prefetch=2,
        grid=(NA_HEADS // NA_HPS, l // NA_TQ),
        in_specs=[
            pl.BlockSpec((NA_TQ, w), lambda h, i, st, pt: (i, QN_COL // NA_HPS + h)),
            pl.BlockSpec((l, w), lambda h, i, st, pt: (0, KN_COL // NA_HPS + h), pipeline_mode=once),
            pl.BlockSpec((l, w), lambda h, i, st, pt: (0, VN_COL // NA_HPS + h), pipeline_mode=once),
            pl.BlockSpec((lc, w), lambda h, i, st, pt: (0, KN_COL // NA_HPS + h)),
            pl.BlockSpec((lc, w), lambda h, i, st, pt: (0, VN_COL // NA_HPS + h)),
            pl.BlockSpec((None, NA_HPS, NA_TQ, NA_SK), lambda h, i, st, pt: (pt[i], h, 0, 0)),
        ],
        out_specs=pl.BlockSpec((NA_TQ, w), lambda h, i, st, pt: (i, h)),
    )
    return pl.pallas_call(
        _na_attn_kernel,
        grid_spec=grid_spec,
        out_shape=jax.ShapeDtypeStruct((l, NA_WIDTH), BF16),
        compiler_params=_params(("parallel", "arbitrary")),
        name="na_attn",
    )(starts, pats, qkv, qkv, qkv, qkvc, qkvc, bias)


def _softmax_pv(q, k, v):
    s = _dot_nt(q, k)
    m = jnp.max(s, axis=-1, keepdims=True)
    p = jnp.exp2(s - m)
    l = jnp.sum(p, axis=-1, keepdims=True)
    return _dot(p.astype(BF16), v) / l


def _ctx_attn_kernel(x_ref, lam_ref, g_ref, od_ref, on_ref, *, lambda_init):
    lam = _lambda_value(lam_ref[...], lambda_init)
    g = g_ref[...]

    def col(base, h, width=HEAD_DIM):
        lo = (base + h) * HEAD_DIM
        return x_ref[:, lo:lo + width]

    for h in range(DA_HEADS):
        v = col(VD_COL, 2 * h, 2 * HEAD_DIM)
        o1 = _softmax_pv(col(QD_COL, 2 * h), col(KD_COL, 2 * h), v)
        o2 = _softmax_pv(col(QD_COL, 2 * h + 1), col(KD_COL, 2 * h + 1), v)
        od_ref[:, h * 256:(h + 1) * 256] = _subln(o1 - lam * o2, g, lambda_init).astype(BF16)
    for h in range(NA_HEADS):
        o = _softmax_pv(col(QN_COL, h), col(KN_COL, h), col(VN_COL, h))
        on_ref[:, h * HEAD_DIM:(h + 1) * HEAD_DIM] = o.astype(BF16)


def _ctx_attn(qkvc, lam_vec, subln_g, lambda_init):
    lc = qkvc.shape[0]
    kern = functools.partial(_ctx_attn_kernel, lambda_init=lambda_init)
    return pl.pallas_call(
        kern,
        grid=(1,),
        in_specs=[
            pl.BlockSpec((lc, IN_DIM), lambda i: (0, 0)),
            pl.BlockSpec((4, HEAD_DIM), lambda i: (0, 0)),
            pl.BlockSpec((1, 2 * HEAD_DIM), lambda i: (0, 0)),
        ],
        out_specs=[pl.BlockSpec((lc, DA_WIDTH), lambda i: (0, 0)),
                   pl.BlockSpec((lc, NA_WIDTH), lambda i: (0, 0))],
        out_shape=[jax.ShapeDtypeStruct((lc, DA_WIDTH), BF16),
                   jax.ShapeDtypeStruct((lc, NA_WIDTH), BF16)],
        compiler_params=_params(("arbitrary",)),
        name="ctx_attn",
    )(qkvc, lam_vec, subln_g)


OUT_SUB = 256


def _out_proj_kernel(od_ref, on_ref, w_ref, x_ref, gate_ref, g_ref, b_ref, o_ref):
    tm = x_ref.shape[0]
    for r0 in range(0, tm, OUT_SUB):
        rows = slice(r0, r0 + OUT_SUB)
        y = _dot(od_ref[rows, :], w_ref[0:DA_WIDTH, :]) + _dot(on_ref[rows, :], w_ref[DA_WIDTH:, :])
        z = ALPHA * x_ref[rows, :] + gate_ref[...] * y
        o_ref[rows, :] = _layernorm_rows(z, g_ref[...], b_ref[...])


def _out_proj_ln(od, on, w, layer, x, gate, ln_g, ln_b, tm):
    m = x.shape[0]
    vec = pl.BlockSpec((1, D_MODEL), lambda i: (0, 0))
    return pl.pallas_call(
        _out_proj_kernel,
        grid=(m // tm,),
        in_specs=[
            pl.BlockSpec((tm, DA_WIDTH), lambda i: (i, 0)),
            pl.BlockSpec((tm, NA_WIDTH), lambda i: (i, 0)),
            pl.BlockSpec((None, DA_WIDTH + NA_WIDTH, D_MODEL), lambda i: (layer, 0, 0),
                         pipeline_mode=pl.Buffered(1)),
            pl.BlockSpec((tm, D_MODEL), lambda i: (i, 0)),
            vec, vec, vec,
        ],
        out_specs=pl.BlockSpec((tm, D_MODEL), lambda i: (i, 0)),
        out_shape=jax.ShapeDtypeStruct((m, D_MODEL), F32),
        compiler_params=_params(("parallel",)),
        name="out_proj_ln",
    )(od, on, w, x, gate, ln_g, ln_b)


FFN_TF = 512
HALO = 16


def _ffn_kernel(x_ref, xp_ref, xn_ref, sc_ref, sh_ref, gate_ref, wg_ref, wu_ref, cw_ref, cb_ref, wd_ref,
                g_ref, b_ref, o_ref, h_ref, acc_ref):
    i = pl.program_id(0)
    c = pl.program_id(1)
    tm = x_ref.shape[0]
    n = tm + 2 * HALO

    @pl.when(c == 0)
    def _():
        sc = 1.0 + sc_ref[...]
        sh = sh_ref[...]
        before = jnp.where(i == 0, 0.0, xp_ref[...] * sc + sh)
        after = jnp.where(i == pl.num_programs(0) - 1, 0.0, xn_ref[...] * sc + sh)
        h_ref[0:HALO, :] = before.astype(BF16)
        h_ref[HALO:HALO + tm, :] = (x_ref[...] * sc + sh).astype(BF16)
        h_ref[HALO + tm:n, :] = after.astype(BF16)
        acc_ref[...] = jnp.zeros_like(acc_ref)

    g_ext = _dot(h_ref[...], wg_ref[...])
    u = _dot(h_ref[HALO:HALO + tm, :], wu_ref[...])
    g = g_ext[HALO:HALO + tm, :]
    g_prev = pltpu.roll(g_ext, 1, 0)[HALO:HALO + tm, :]
    g_next = pltpu.roll(g_ext, n - 1, 0)[HALO:HALO + tm, :]
    gc = g_prev * cw_ref[0:1, :] + g * cw_ref[1:2, :] + g_next * cw_ref[2:3, :] + cb_ref[...]
    act = gc / (1.0 + jnp.exp(-gc)) * u
    acc_ref[...] += _dot(act.astype(BF16), wd_ref[...])

    @pl.when(c == pl.num_programs(1) - 1)
    def _():
        z = ALPHA * x_ref[...] + gate_ref[...] * acc_ref[...]
        o_ref[...] = _layernorm_rows(z, g_ref[...], b_ref[...])


def _ffn_ln(x, sc, sh, gate, w_up, conv_w, conv_b, w_down, layer, ln_g, ln_b, tm):
    m = x.shape[0]
    nf = D_FF // FFN_TF
    tb = tm // HALO
    last_blk = m // HALO - 1
    vec = pl.BlockSpec((1, D_MODEL), lambda i, c: (0, 0))
    return pl.pallas_call(
        _ffn_kernel,
        grid=(m // tm, nf),
        in_specs=[
            pl.BlockSpec((tm, D_MODEL), lambda i, c: (i, 0)),
            pl.BlockSpec((HALO, D_MODEL), lambda i, c: (jnp.maximum(i * tb - 1, 0), 0)),
            pl.BlockSpec((HALO, D_MODEL), lambda i, c: (jnp.minimum((i + 1) * tb, last_blk), 0)),
            vec, vec, vec,
            pl.BlockSpec((None, D_MODEL, FFN_TF), lambda i, c: (layer, 0, c)),
            pl.BlockSpec((None, D_MODEL, FFN_TF), lambda i, c: (layer, 0, nf + c)),
            pl.BlockSpec((3, FFN_TF), lambda i, c: (0, c)),
            pl.BlockSpec((1, FFN_TF), lambda i, c: (0, c)),
            pl.BlockSpec((None, FFN_TF, D_MODEL), lambda i, c: (layer, c, 0)),
            vec, vec,
        ],
        out_specs=pl.BlockSpec((tm, D_MODEL), lambda i, c: (i, 0)),
        out_shape=jax.ShapeDtypeStruct((m, D_MODEL), F32),
        scratch_shapes=[pltpu.VMEM((tm + 2 * HALO, D_MODEL), BF16),
                        pltpu.VMEM((tm, D_MODEL), F32)],
        compiler_params=_params(("parallel", "arbitrary")),
        name="ffn_ln",
    )(x, x, x, sc, sh, gate, w_up, w_up, conv_w, conv_b, w_down, ln_g, ln_b)


def _rope_tables(l):
    half = HEAD_DIM // 2
    inv_freq = 1.0 / (ROPE_THETA ** (jnp.arange(0, half, 2, dtype=F32) / half))
    zeros = jnp.zeros((1, half), F32)

    def tables(pos, row_part):
        ang = pos.astype(F32)[:, None] * inv_freq[None, :]
        ang = jnp.concatenate([ang, ang], axis=-1)
        cos, sin = jnp.cos(ang), jnp.sin(ang)
        first = jnp.asarray(np.arange(half) < half // 2)[None, :]
        sin_a = jnp.where(first, -sin, 0.0)
        sin_b = jnp.where(first, 0.0, sin)
        pad = jnp.broadcast_to(zeros, cos.shape)
        order = (lambda t: jnp.concatenate([t, pad], axis=-1)) if row_part else \
                (lambda t: jnp.concatenate([pad, t], axis=-1))
        return order(cos), order(sin_a), order(sin_b)

    rows = tables(jnp.arange(l // GRID_W, dtype=jnp.int32), True)
    cols = tables(jnp.arange(GRID_W, dtype=jnp.int32), False)
    return rows + cols


def _identity_rope_tables(lc):
    nrow = lc // GRID_W
    lanes = jnp.asarray(np.arange(HEAD_DIM) < HEAD_DIM // 2, F32)[None, :]
    zr = jnp.zeros((nrow, HEAD_DIM), F32)
    zc = jnp.zeros((GRID_W, HEAD_DIM), F32)
    return (jnp.broadcast_to(lanes, (nrow, HEAD_DIM)), zr, zr,
            jnp.broadcast_to(1.0 - lanes, (GRID_W, HEAD_DIM)), zc, zc)


def kernel(x, c, ctx, c_ctx, w_ada, b_ada, w_in, da_lambda, da_subln, na_rpb, w_o,
           ln1_g, ln1_b, w_up, conv_w, conv_b, w_down, ln2_g, ln2_b):
    assert x.shape[0] == 1 and ctx.shape[0] == 1
    l = x.shape[1]
    lc = ctx.shape[1]
    xs = x[0]
    xc = ctx[0]

    cond = jnp.broadcast_to(jnp.stack([c[0], c_ctx])[:, :, None], (2, D_MODEL, 128))
    mods = _ada_mod(cond, w_ada, b_ada)

    rope = _rope_tables(l)
    rope_ctx = _identity_rope_tables(lc)
    w_in_b = w_in.astype(BF16)
    w_o_b = w_o.astype(BF16)
    w_up_b = w_up.astype(BF16)
    w_down_b = w_down.astype(BF16)
    starts, pats, geoms = _na_plan(l)
    starts = jnp.asarray(starts)
    pats = jnp.asarray(pats)

    for layer in range(DEPTH):
        last = layer == DEPTH - 1
        lambda_init = 0.8 - 0.6 * math.exp(-0.3 * layer)
        sh_a, sc_a, g_a, sh_m, sc_m, g_m = jnp.split(mods[layer, 0:1], N_MOD, axis=-1)
        shc_a, scc_a, gc_a, shc_m, scc_m, gc_m = jnp.split(mods[layer, 1:2], N_MOD, axis=-1)
        lam_vec = da_lambda[layer]
        subln_g = da_subln[layer].reshape(1, 2 * HEAD_DIM)
        ln1 = (ln1_g[layer].reshape(1, D_MODEL), ln1_b[layer].reshape(1, D_MODEL))
        ln2 = (ln2_g[layer].reshape(1, D_MODEL), ln2_b[layer].reshape(1, D_MODEL))
        cw = conv_w[layer]
        cb = conv_b[layer].reshape(1, D_FF)

        qkv = _in_proj(xs, sc_a, sh_a, w_in_b, layer, *rope, tm=256)
        qkvc = _in_proj(xc, scc_a, shc_a, w_in_b, layer, *rope_ctx, tm=lc)

        od = _diff_attn(qkv, qkvc, lam_vec, subln_g, lambda_init, tq=512, tk=512)
        bias = _na_bias_table(na_rpb[layer], geoms)
        on = _na_attn(qkv, qkvc, bias, starts, pats)

        xs = _out_proj_ln(od, on, w_o_b, layer, xs, g_a, *ln1, tm=512)
        xs = _ffn_ln(xs, sc_m, sh_m, g_m, w_up_b, cw, cb, w_down_b, layer, *ln2, tm=512)

        if not last:
            odc, onc = _ctx_attn(qkvc, lam_vec, subln_g, lambda_init)
            xc = _out_proj_ln(odc, onc, w_o_b, layer, xc, gc_a, *ln1, tm=lc)
            xc = _ffn_ln(xc, scc_m, shc_m, gc_m, w_up_b, cw, cb, w_down_b, layer, *ln2, tm=lc)

    return xs[None]
```

```python
import functools
import math

import numpy as np
import jax
import jax.numpy as jnp
from jax import lax
from jax.experimental import pallas as pl
from jax.experimental.pallas import tpu as pltpu

D_MODEL = 2048
DEPTH = 2
GRID_W = 64
HEAD_DIM = 128
DA_HEADS = 4
NA_HEADS = 8
DA_WIDTH = DA_HEADS * 2 * HEAD_DIM
NA_WIDTH = NA_HEADS * HEAD_DIM
IN_DIM = 3 * DA_WIDTH + 3 * NA_WIDTH
NA_KH = 8
NA_KW = 16
D_FF = 5632
ROPE_THETA = 10000.0
LN_EPS = 1e-5
N_MOD = 6
ALPHA = (2.0 * DEPTH) ** 0.25
LOG2E = math.log2(math.e)
Q_SCALE = (HEAD_DIM ** -0.5) * LOG2E
MASK_VALUE = -1e30

F32 = jnp.float32
BF16 = jnp.bfloat16

VMEM_LIMIT = 56 * 1024 * 1024

QD_COL, KD_COL, VD_COL = 0, DA_WIDTH // 128, 2 * DA_WIDTH // 128
QN_COL = 3 * DA_WIDTH // 128
KN_COL = QN_COL + NA_WIDTH // 128
VN_COL = KN_COL + NA_WIDTH // 128


def _params(semantics):
    return pltpu.CompilerParams(dimension_semantics=semantics, vmem_limit_bytes=VMEM_LIMIT)


def _dot(a, b):
    return jnp.dot(a, b, preferred_element_type=F32)


def _dot_nt(a, b):
    return lax.dot_general(a, b, (((1,), (1,)), ((), ())), preferred_element_type=F32)


def _layernorm_rows(z, g, b):
    mu = jnp.mean(z, axis=-1, keepdims=True)
    zc = z - mu
    var = jnp.mean(zc * zc, axis=-1, keepdims=True)
    return zc * lax.rsqrt(var + LN_EPS) * g + b


ADA_TN = 1024
ADA_KC = 256


def _ada_kernel(c_ref, w_ref, b_ref, o_ref):
    reps = w_ref.shape[1] // 128
    for r in range(c_ref.shape[0]):
        acc = b_ref[...]
        for k0 in range(0, D_MODEL, ADA_KC):
            a = c_ref[r, k0:k0 + ADA_KC, :]
            silu = a / (1.0 + jnp.exp(-a))
            sw = jnp.concatenate([silu] * reps, axis=1)
            acc = acc + jnp.sum(w_ref[k0:k0 + ADA_KC, :] * sw, axis=0, keepdims=True)
        o_ref[r:r + 1, :] = acc


def _ada_mod(cond, w_ada, b_ada):
    nmod = N_MOD * D_MODEL
    nrow = cond.shape[0]
    return pl.pallas_call(
        _ada_kernel,
        grid=(DEPTH, nmod // ADA_TN),
        in_specs=[
            pl.BlockSpec((nrow, D_MODEL, 128), lambda l, j: (0, 0, 0)),
            pl.BlockSpec((None, D_MODEL, ADA_TN), lambda l, j: (l, 0, j)),
            pl.BlockSpec((None, 1, ADA_TN), lambda l, j: (l, 0, j)),
        ],
        out_specs=pl.BlockSpec((None, nrow, ADA_TN), lambda l, j: (l, 0, j)),
        out_shape=jax.ShapeDtypeStruct((DEPTH, nrow, nmod), F32),
        compiler_params=_params(("parallel", "parallel")),
        name="ada_mod",
    )(cond, w_ada, b_ada.reshape(DEPTH, 1, nmod))


PROJ_TN = 1024


def _rope_tile(row_ref, col_ref, i, tm):
    rows_per_tile = tm // GRID_W
    parts = [jnp.broadcast_to(row_ref[pl.ds(i * rows_per_tile + k, 1), :], (GRID_W, HEAD_DIM)) + col_ref[...]
             for k in range(rows_per_tile)]
    return jnp.concatenate(parts, axis=0)


def _in_proj_kernel(x_ref, sc_ref, sh_ref, w_ref, cosr_ref, sar_ref, sbr_ref, cosc_ref, sac_ref, sbc_ref,
                    o_ref, h_ref):
    i = pl.program_id(0)
    tm = x_ref.shape[0]
    h_ref[...] = (x_ref[...] * (1.0 + sc_ref[...]) + sh_ref[...]).astype(BF16)
    cos_t = _rope_tile(cosr_ref, cosc_ref, i, tm)
    sa_t = _rope_tile(sar_ref, sac_ref, i, tm)
    sb_t = _rope_tile(sbr_ref, sbc_ref, i, tm)
    for j in range(IN_DIM // PROJ_TN):
        acc = _dot(h_ref[...], w_ref[:, j * PROJ_TN:(j + 1) * PROJ_TN])
        if j <= 1:
            scale = Q_SCALE if j == 0 else 1.0
            cos = cos_t * scale
            sa = sa_t * scale
            sb = sb_t * scale
            for k in range(PROJ_TN // HEAD_DIM):
                a = acc[:, k * HEAD_DIM:(k + 1) * HEAD_DIM]
                r = a * cos + pltpu.roll(a, 96, 1) * sa + pltpu.roll(a, 32, 1) * sb
                o_ref[:, j * PROJ_TN + k * HEAD_DIM:j * PROJ_TN + (k + 1) * HEAD_DIM] = r.astype(BF16)
        elif j == 3:
            o_ref[:, j * PROJ_TN:(j + 1) * PROJ_TN] = (acc * Q_SCALE).astype(BF16)
        else:
            o_ref[:, j * PROJ_TN:(j + 1) * PROJ_TN] = acc.astype(BF16)


def _in_proj(x, sc, sh, w, layer, cos_r, sa_r, sb_r, cos_c, sa_c, sb_c, tm):
    m = x.shape[0]
    nrow = cos_r.shape[0]
    tab_r = pl.BlockSpec((nrow, HEAD_DIM), lambda i: (0, 0))
    tab_c = pl.BlockSpec((GRID_W, HEAD_DIM), lambda i: (0, 0))
    return pl.pallas_call(
        _in_proj_kernel,
        grid=(m // tm,),
        in_specs=[
            pl.BlockSpec((tm, D_MODEL), lambda i: (i, 0)),
            pl.BlockSpec((1, D_MODEL), lambda i: (0, 0)),
            pl.BlockSpec((1, D_MODEL), lambda i: (0, 0)),
            pl.BlockSpec((None, D_MODEL, IN_DIM), lambda i: (layer, 0, 0), pipeline_mode=pl.Buffered(1)),
            tab_r, tab_r, tab_r, tab_c, tab_c, tab_c,
        ],
        out_specs=pl.BlockSpec((tm, IN_DIM), lambda i: (i, 0)),
        out_shape=jax.ShapeDtypeStruct((m, IN_DIM), BF16),
        scratch_shapes=[pltpu.VMEM((tm, D_MODEL), BF16)],
        compiler_params=_params(("parallel",)),
        name="in_proj",
    )(x, sc, sh, w, cos_r, sa_r, sb_r, cos_c, sa_c, sb_c)


UNROLL = 8


def _lambda_value(lv, lambda_init):
    a = jnp.sum(lv[0:1, :] * lv[1:2, :], axis=-1, keepdims=True)
    b = jnp.sum(lv[2:3, :] * lv[3:4, :], axis=-1, keepdims=True)
    return jnp.exp(a) - jnp.exp(b) + lambda_init


def _subln(o, g, lambda_init):
    ms = jnp.mean(o * o, axis=-1, keepdims=True)
    return o * lax.rsqrt(ms + LN_EPS) * g * (1.0 - lambda_init)


def _diff_attn_kernel(q_ref, qn_ref, k_ref, v_ref, kc_ref, vc_ref, lam_ref, g_ref, o_ref,
                      vt_ref, qt_ref, mx_ref, acc1_ref, acc2_ref, sa1_ref, sa2_ref, sb1_ref, sb2_ref,
                      *, tk, lambda_init):
    nk = k_ref.shape[0] // tk
    accs = (acc1_ref, acc2_ref)
    bufs = ((sa1_ref, sa2_ref), (sb1_ref, sb2_ref))
    first = pl.program_id(1) == 0
    unroll = math.gcd(nk, UNROLL)
    assert unroll % 2 == 0
    trips = nk // unroll

    def scores_into(i, dst, slot):
        off = pl.multiple_of(i * tk, tk)
        mx = []
        for c in range(2):
            cols = slice(c * HEAD_DIM, (c + 1) * HEAD_DIM)
            s = _dot(k_ref[pl.ds(off, tk), cols], qt_ref[slot, cols, :])
            dst[c][...] = s
            mx.append(jnp.max(s, axis=0, keepdims=True))
        return tuple(mx)

    @pl.when(first)
    def _():
        def fill(i, carry):
            off = pl.multiple_of(i * tk, tk)
            vt_ref[i] = v_ref[pl.ds(off, tk), :].astype(F32).T.astype(BF16)
            return carry
        lax.fori_loop(0, nk, fill, 0)
        qt_ref[0] = q_ref[...].astype(F32).T.astype(BF16)
        mx0 = scores_into(0, bufs[0], 0)
        mx_ref[0:1, :] = mx0[0]
        mx_ref[1:2, :] = mx0[1]

    @pl.when(jnp.logical_not(first))
    def _():
        qt_ref[0] = qt_ref[1]

    def softmax_pv(i, src, mx, carry):
        vt = vt_ref[i]
        out = []
        for c in range(2):
            if carry is None:
                mn = mx[c]
                p = jnp.exp2(src[c][...] - mn)
                l = jnp.sum(p, axis=0, keepdims=True)
                accs[c][...] = _dot(vt, p.astype(BF16))
            else:
                m, l = carry[2 * c], carry[2 * c + 1]
                mn = jnp.maximum(m, mx[c])
                a = jnp.exp2(m - mn)
                p = jnp.exp2(src[c][...] - mn)
                l = a * l + jnp.sum(p, axis=0, keepdims=True)
                accs[c][...] = a * accs[c][...] + _dot(vt, p.astype(BF16))
            out += [mn, l]
        return tuple(out)

    def context_step(carry):
        vc_t = vc_ref[...].astype(F32).T.astype(BF16)
        out = []
        for c in range(2):
            cols = slice(c * HEAD_DIM, (c + 1) * HEAD_DIM)
            m, l = carry[2 * c], carry[2 * c + 1]
            s = _dot(kc_ref[:, cols], qt_ref[0, cols, :])
            mn = jnp.maximum(m, jnp.max(s, axis=0, keepdims=True))
            a = jnp.exp2(m - mn)
            p = jnp.exp2(s - mn)
            l = a * l + jnp.sum(p, axis=0, keepdims=True)
            accs[c][...] = a * accs[c][...] + _dot(vc_t, p.astype(BF16))
            out += [mn, l]
        return tuple(out)

    def blocks(trip, mx, carry):
        base = trip * unroll
        for u in range(unroll):
            if u == unroll - 1:
                wrap = trip == trips - 1
                nxt = jnp.where(wrap, 0, base + unroll)
                mx_next = scores_into(nxt, bufs[(u + 1) % 2], jnp.where(wrap, 1, 0))
            else:
                mx_next = scores_into(base + u + 1, bufs[(u + 1) % 2], 0)
            start = carry is None
            carry = softmax_pv(base + u, bufs[u % 2], mx, carry)
            if start:
                carry = context_step(carry)
                qt_ref[1] = qn_ref[...].astype(F32).T.astype(BF16)
            mx = mx_next
        return mx, carry

    state = blocks(0, (mx_ref[0:1, :], mx_ref[1:2, :]), None)

    mx_n, (m1, l1, m2, l2) = lax.fori_loop(1, trips, lambda j, st: blocks(j, st[0], st[1]), state)
    mx_ref[0:1, :] = mx_n[0]
    mx_ref[1:2, :] = mx_n[1]

    lam = _lambda_value(lam_ref[...], lambda_init)
    o_t = acc1_ref[...] / l1 - lam * (acc2_ref[...] / l2)
    ms = jnp.mean(o_t * o_t, axis=0, keepdims=True)
    o = (o_t * lax.rsqrt(ms + LN_EPS)).T
    o_ref[...] = (o * g_ref[...] * (1.0 - lambda_init)).astype(BF16)


def _diff_attn(qkv, qkvc, lam_vec, subln_g, lambda_init, tq, tk):
    l = qkv.shape[0]
    lc = qkvc.shape[0]
    kern = functools.partial(_diff_attn_kernel, tk=tk, lambda_init=lambda_init)
    return pl.pallas_call(
        kern,
        grid=(DA_HEADS, l // tq),
        in_specs=[
            pl.BlockSpec((tq, 256), lambda h, i: (i, QD_COL // 2 + h)),
            pl.BlockSpec((tq, 256), lambda h, i: (jnp.minimum(i + 1, l // tq - 1), QD_COL // 2 + h)),
            pl.BlockSpec((l, 256), lambda h, i: (0, KD_COL // 2 + h)),
            pl.BlockSpec((l, 256), lambda h, i: (0, VD_COL // 2 + h)),
            pl.BlockSpec((lc, 256), lambda h, i: (0, KD_COL // 2 + h)),
            pl.BlockSpec((lc, 256), lambda h, i: (0, VD_COL // 2 + h)),
            pl.BlockSpec((4, HEAD_DIM), lambda h, i: (0, 0)),
            pl.BlockSpec((1, 2 * HEAD_DIM), lambda h, i: (0, 0)),
        ],
        out_specs=pl.BlockSpec((tq, 256), lambda h, i: (i, h)),
        out_shape=jax.ShapeDtypeStruct((l, DA_WIDTH), BF16),
        scratch_shapes=[pltpu.VMEM((l // tk, 256, tk), BF16),
                        pltpu.VMEM((2, 256, tq), BF16), pltpu.VMEM((8, tq), F32),
                        pltpu.VMEM((256, tq), F32), pltpu.VMEM((256, tq), F32)]
        + [pltpu.VMEM((tk, tq), F32)] * 4,
        compiler_params=_params(("parallel", "arbitrary")),
        name="diff_attn",
    )(qkv, qkv, qkv, qkv, qkvc, qkvc, lam_vec, subln_g)


NA_ROWS = 4
NA_TQ = NA_ROWS * GRID_W
NA_SLAB = NA_ROWS + NA_KH - 1
NA_SK = NA_SLAB * GRID_W


def _na_plan(l):
    rows = l // GRID_W
    kh = min(NA_KH, rows)
    assert kh == NA_KH and rows >= NA_SLAB and rows % NA_ROWS == 0
    nblk = rows // NA_ROWS
    starts, pats, geoms = [], [], []
    for b in range(nblk):
        r0 = b * NA_ROWS
        kstart = int(np.clip(r0 - kh // 2, 0, rows - NA_SLAB))
        rs = [int(np.clip(r0 + qr - kh // 2, 0, rows - kh)) for qr in range(NA_ROWS)]
        geom = (r0 - kstart, tuple(x - kstart for x in rs))
        assert all(0 <= x and x + kh <= NA_SLAB for x in geom[1])
        if geom not in geoms:
            geoms.append(geom)
        starts.append(kstart * GRID_W)
        pats.append(geoms.index(geom))
    return np.asarray(starts, np.int32), np.asarray(pats, np.int32), geoms


def _na_bias_kernel(t2_ref, o_ref, *, geoms):
    p = pl.program_id(1)
    lane = lax.broadcasted_iota(jnp.int32, (GRID_W, 2 * GRID_W), 1)
    for ps, (dq, rs) in enumerate(geoms):
        @pl.when(p == ps)
        def _(dq=dq, rs=rs):
            for qr in range(NA_ROWS):
                def inside(j, qr=qr):
                    return j < NA_SLAB and rs[qr] <= j < rs[qr] + NA_KH
                for j0 in range(0, NA_SLAB, 2):
                    left, right = inside(j0), inside(j0 + 1)
                    d0 = j0 - (dq + qr) + (NA_KH - 1)
                    if left or right:
                        blk = t2_ref[d0 + 1]
                        if not left:
                            blk = jnp.where(lane < GRID_W, MASK_VALUE, blk)
                        if not right:
                            blk = jnp.where(lane < GRID_W, blk, MASK_VALUE)
                    else:
                        blk = jnp.full((GRID_W, 2 * GRID_W), MASK_VALUE, F32)
                    width = min(2, NA_SLAB - j0) * GRID_W
                    o_ref[qr * GRID_W:(qr + 1) * GRID_W, j0 * GRID_W:j0 * GRID_W + width] = blk[:, :width]


def _na_bias_table(rpb, geoms):
    c = np.arange(GRID_W)
    dc = c[None, :] - c[:, None] + (NA_KW - 1)
    cs = np.clip(c - NA_KW // 2, 0, GRID_W - NA_KW)
    col_ok = (c[None, :] >= cs[:, None]) & (c[None, :] < cs[:, None] + NA_KW)
    e_col = np.zeros((2 * NA_KW - 1, GRID_W, GRID_W), np.float32)
    for ci in range(GRID_W):
        for ki in range(GRID_W):
            if col_ok[ci, ki]:
                e_col[dc[ci, ki], ci, ki] = 1.0
    t = jnp.einsum('hrd,dck->hrck', rpb.astype(F32), jnp.asarray(e_col), precision=lax.Precision.HIGHEST)
    t = jnp.where(jnp.asarray(col_ok)[None, None], t * LOG2E, MASK_VALUE)
    tp = jnp.pad(t, ((0, 0), (1, 1), (0, 0), (0, 0)))
    t2 = jnp.concatenate([tp[:, :-1], tp[:, 1:]], axis=-1)
    npat = len(geoms)
    return pl.pallas_call(
        functools.partial(_na_bias_kernel, geoms=tuple(geoms)),
        grid=(NA_HEADS, npat),
        in_specs=[pl.BlockSpec((None, 2 * NA_KH, GRID_W, 2 * GRID_W), lambda h, p: (h, 0, 0, 0))],
        out_specs=pl.BlockSpec((None, None, NA_TQ, NA_SK), lambda h, p: (p, h, 0, 0)),
        out_shape=jax.ShapeDtypeStruct((npat, NA_HEADS, NA_TQ, NA_SK), F32),
        compiler_params=_params(("parallel", "arbitrary")),
        name="na_bias",
    )(t2)


NA_HPS = 4


def _na_attn_kernel(start_ref, pat_ref, q_ref, k_ref, v_ref, kc_ref, vc_ref, bias_ref, o_ref):
    del pat_ref
    i = pl.program_id(1)
    off = pl.multiple_of(start_ref[i], GRID_W)
    for h in range(NA_HPS):
        cols = slice(h * HEAD_DIM, (h + 1) * HEAD_DIM)
        q = q_ref[:, cols]
        s_loc = _dot_nt(q, k_ref[pl.ds(off, NA_SK), cols]) + bias_ref[h]
        s_ctx = _dot_nt(q, kc_ref[:, cols])
        m = jnp.maximum(jnp.max(s_loc, axis=-1, keepdims=True), jnp.max(s_ctx, axis=-1, keepdims=True))
        p_loc = jnp.exp2(s_loc - m)
        p_ctx = jnp.exp2(s_ctx - m)
        l = jnp.sum(p_loc, axis=-1, keepdims=True) + jnp.sum(p_ctx, axis=-1, keepdims=True)
        o = (_dot(p_loc.astype(BF16), v_ref[pl.ds(off, NA_SK), cols])
             + _dot(p_ctx.astype(BF16), vc_ref[:, cols]))
        o_ref[:, cols] = (o / l).astype(BF16)


def _na_attn(qkv, qkvc, bias, starts, pats):
    l = qkv.shape[0]
    lc = qkvc.shape[0]
    w = NA_HPS * HEAD_DIM
    once = pl.Buffered(1)
    grid_spec = pltpu.PrefetchScalarGridSpec(
        num_scalar_prefetch=2,
        grid=(NA_HEADS // NA_HPS, l // NA_TQ),
        in_specs=[
            pl.BlockSpec((NA_TQ, w), lambda h, i, st, pt: (i, QN_COL // NA_HPS + h)),
            pl.BlockSpec((l, w), lambda h, i, st, pt: (0, KN_COL // NA_HPS + h), pipeline_mode=once),
            pl.BlockSpec((l, w), lambda h, i, st, pt: (0, VN_COL // NA_HPS + h), pipeline_mode=once),
            pl.BlockSpec((lc, w), lambda h, i, st, pt: (0, KN_COL // NA_HPS + h)),
            pl.BlockSpec((lc, w), lambda h, i, st, pt: (0, VN_COL // NA_HPS + h)),
            pl.BlockSpec((None, NA_HPS, NA_TQ, NA_SK), lambda h, i, st, pt: (pt[i], h, 0, 0)),
        ],
        out_specs=pl.BlockSpec((NA_TQ, w), lambda h, i, st, pt: (i, h)),
    )
    return pl.pallas_call(
        _na_attn_kernel,
        grid_spec=grid_spec,
        out_shape=jax.ShapeDtypeStruct((l, NA_WIDTH), BF16),
        compiler_params=_params(("parallel", "arbitrary")),
        name="na_attn",
    )(starts, pats, qkv, qkv, qkv, qkvc, qkvc, bias)


def _softmax_pv(q, k, v):
    s = _dot_nt(q, k)
    m = jnp.max(s, axis=-1, keepdims=True)
    p = jnp.exp2(s - m)
    l = jnp.sum(p, axis=-1, keepdims=True)
    return _dot(p.astype(BF16), v) / l


def _ctx_attn_kernel(x_ref, lam_ref, g_ref, od_ref, on_ref, *, lambda_init):
    lam = _lambda_value(lam_ref[...], lambda_init)
    g = g_ref[...]

    def col(base, h, width=HEAD_DIM):
        lo = (base + h) * HEAD_DIM
        return x_ref[:, lo:lo + width]

    for h in range(DA_HEADS):
        v = col(VD_COL, 2 * h, 2 * HEAD_DIM)
        o1 = _softmax_pv(col(QD_COL, 2 * h), col(KD_COL, 2 * h), v)
        o2 = _softmax_pv(col(QD_COL, 2 * h + 1), col(KD_COL, 2 * h + 1), v)
        od_ref[:, h * 256:(h + 1) * 256] = _subln(o1 - lam * o2, g, lambda_init).astype(BF16)
    for h in range(NA_HEADS):
        o = _softmax_pv(col(QN_COL, h), col(KN_COL, h), col(VN_COL, h))
        on_ref[:, h * HEAD_DIM:(h + 1) * HEAD_DIM] = o.astype(BF16)


def _ctx_attn(qkvc, lam_vec, subln_g, lambda_init):
    lc = qkvc.shape[0]
    kern = functools.partial(_ctx_attn_kernel, lambda_init=lambda_init)
    return pl.pallas_call(
        kern,
        grid=(1,),
        in_specs=[
            pl.BlockSpec((lc, IN_DIM), lambda i: (0, 0)),
            pl.BlockSpec((4, HEAD_DIM), lambda i: (0, 0)),
            pl.BlockSpec((1, 2 * HEAD_DIM), lambda i: (0, 0)),
        ],
        out_specs=[pl.BlockSpec((lc, DA_WIDTH), lambda i: (0, 0)),
                   pl.BlockSpec((lc, NA_WIDTH), lambda i: (0, 0))],
        out_shape=[jax.ShapeDtypeStruct((lc, DA_WIDTH), BF16),
                   jax.ShapeDtypeStruct((lc, NA_WIDTH), BF16)],
        compiler_params=_params(("arbitrary",)),
        name="ctx_attn",
    )(qkvc, lam_vec, subln_g)


OUT_SUB = 256


def _out_proj_kernel(od_ref, on_ref, w_ref, x_ref, gate_ref, g_ref, b_ref, o_ref):
    tm = x_ref.shape[0]
    for r0 in range(0, tm, OUT_SUB):
        rows = slice(r0, r0 + OUT_SUB)
        y = _dot(od_ref[rows, :], w_ref[0:DA_WIDTH, :]) + _dot(on_ref[rows, :], w_ref[DA_WIDTH:, :])
        z = ALPHA * x_ref[rows, :] + gate_ref[...] * y
        o_ref[rows, :] = _layernorm_rows(z, g_ref[...], b_ref[...])


def _out_proj_ln(od, on, w, layer, x, gate, ln_g, ln_b, tm):
    m = x.shape[0]
    vec = pl.BlockSpec((1, D_MODEL), lambda i: (0, 0))
    return pl.pallas_call(
        _out_proj_kernel,
        grid=(m // tm,),
        in_specs=[
            pl.BlockSpec((tm, DA_WIDTH), lambda i: (i, 0)),
            pl.BlockSpec((tm, NA_WIDTH), lambda i: (i, 0)),
            pl.BlockSpec((None, DA_WIDTH + NA_WIDTH, D_MODEL), lambda i: (layer, 0, 0),
                         pipeline_mode=pl.Buffered(1)),
            pl.BlockSpec((tm, D_MODEL), lambda i: (i, 0)),
            vec, vec, vec,
        ],
        out_specs=pl.BlockSpec((tm, D_MODEL), lambda i: (i, 0)),
        out_shape=jax.ShapeDtypeStruct((m, D_MODEL), F32),
        compiler_params=_params(("parallel",)),
        name="out_proj_ln",
    )(od, on, w, x, gate, ln_g, ln_b)


FFN_TF = 512
HALO = 16


def _ffn_kernel(x_ref, xp_ref, xn_ref, sc_ref, sh_ref, gate_ref, wg_ref, wu_ref, cw_ref, cb_ref, wd_ref,
                g_ref, b_ref, o_ref, h_ref, acc_ref):
    i = pl.program_id(0)
    c = pl.program_id(1)
    tm = x_ref.shape[0]
    n = tm + HALO

    @pl.when(c == 0)
    def _():
        sc = 1.0 + sc_ref[...]
        sh = sh_ref[...]
        before = jnp.where(i == 0, 0.0, xp_ref[HALO - 1:HALO, :] * sc + sh)
        after = jnp.where(i == pl.num_programs(0) - 1, 0.0, xn_ref[0:1, :] * sc + sh)
        row = lax.broadcasted_iota(jnp.int32, (HALO, D_MODEL), 0)
        tail = jnp.where(row == 0, after, jnp.where(row == HALO - 1, before, 0.0))
        h_ref[0:tm, :] = (x_ref[...] * sc + sh).astype(BF16)
        h_ref[tm:n, :] = tail.astype(BF16)
        acc_ref[...] = jnp.zeros_like(acc_ref)

    g_ext = _dot(h_ref[...], wg_ref[...])
    u = _dot(h_ref[0:tm, :], wu_ref[...])
    g = g_ext[0:tm, :]
    g_prev = pltpu.roll(g_ext, 1, 0)[0:tm, :]
    g_next = pltpu.roll(g_ext, n - 1, 0)[0:tm, :]
    gc = g_prev * cw_ref[0:1, :] + g * cw_ref[1:2, :] + g_next * cw_ref[2:3, :] + cb_ref[...]
    act = gc / (1.0 + jnp.exp(-gc)) * u
    acc_ref[...] += _dot(act.astype(BF16), wd_ref[...])

    @pl.when(c == pl.num_programs(1) - 1)
    def _():
        z = ALPHA * x_ref[...] + gate_ref[...] * acc_ref[...]
        o_ref[...] = _layernorm_rows(z, g_ref[...], b_ref[...])


def _ffn_ln(x, sc, sh, gate, w_up, conv_w, conv_b, w_down, layer, ln_g, ln_b, tm):
    m = x.shape[0]
    nf = D_FF // FFN_TF
    tb = tm // HALO
    last_blk = m // HALO - 1
    vec = pl.BlockSpec((1, D_MODEL), lambda i, c: (0, 0))
    return pl.pallas_call(
        _ffn_kernel,
        grid=(m // tm, nf),
        in_specs=[
            pl.BlockSpec((tm, D_MODEL), lambda i, c: (i, 0)),
            pl.BlockSpec((HALO, D_MODEL), lambda i, c: (jnp.maximum(i * tb - 1, 0), 0)),
            pl.BlockSpec((HALO, D_MODEL), lambda i, c: (jnp.minimum((i + 1) * tb, last_blk), 0)),
            vec, vec, vec,
            pl.BlockSpec((None, D_MODEL, FFN_TF), lambda i, c: (layer, 0, c)),
            pl.BlockSpec((None, D_MODEL, FFN_TF), lambda i, c: (layer, 0, nf + c)),
            pl.BlockSpec((3, FFN_TF), lambda i, c: (0, c)),
            pl.BlockSpec((1, FFN_TF), lambda i, c: (0, c)),
            pl.BlockSpec((None, FFN_TF, D_MODEL), lambda i, c: (layer, c, 0)),
            vec, vec,
        ],
        out_specs=pl.BlockSpec((tm, D_MODEL), lambda i, c: (i, 0)),
        out_shape=jax.ShapeDtypeStruct((m, D_MODEL), F32),
        scratch_shapes=[pltpu.VMEM((tm + HALO, D_MODEL), BF16),
                        pltpu.VMEM((tm, D_MODEL), F32)],
        compiler_params=_params(("parallel", "arbitrary")),
        name="ffn_ln",
    )(x, x, x, sc, sh, gate, w_up, w_up, conv_w, conv_b, w_down, ln_g, ln_b)


def _rope_tables(l):
    half = HEAD_DIM // 2
    inv_freq = 1.0 / (ROPE_THETA ** (jnp.arange(0, half, 2, dtype=F32) / half))
    zeros = jnp.zeros((1, half), F32)

    def tables(pos, row_part):
        ang = pos.astype(F32)[:, None] * inv_freq[None, :]
        ang = jnp.concatenate([ang, ang], axis=-1)
        cos, sin = jnp.cos(ang), jnp.sin(ang)
        first = jnp.asarray(np.arange(half) < half // 2)[None, :]
        sin_a = jnp.where(first, -sin, 0.0)
        sin_b = jnp.where(first, 0.0, sin)
        pad = jnp.broadcast_to(zeros, cos.shape)
        order = (lambda t: jnp.concatenate([t, pad], axis=-1)) if row_part else \
                (lambda t: jnp.concatenate([pad, t], axis=-1))
        return order(cos), order(sin_a), order(sin_b)

    rows = tables(jnp.arange(l // GRID_W, dtype=jnp.int32), True)
    cols = tables(jnp.arange(GRID_W, dtype=jnp.int32), False)
    return rows + cols


def _identity_rope_tables(lc):
    nrow = lc // GRID_W
    lanes = jnp.asarray(np.arange(HEAD_DIM) < HEAD_DIM // 2, F32)[None, :]
    zr = jnp.zeros((nrow, HEAD_DIM), F32)
    zc = jnp.zeros((GRID_W, HEAD_DIM), F32)
    return (jnp.broadcast_to(lanes, (nrow, HEAD_DIM)), zr, zr,
            jnp.broadcast_to(1.0 - lanes, (GRID_W, HEAD_DIM)), zc, zc)


def kernel(x, c, ctx, c_ctx, w_ada, b_ada, w_in, da_lambda, da_subln, na_rpb, w_o,
           ln1_g, ln1_b, w_up, conv_w, conv_b, w_down, ln2_g, ln2_b):
    assert x.shape[0] == 1 and ctx.shape[0] == 1
    l = x.shape[1]
    lc = ctx.shape[1]
    xs = x[0]
    xc = ctx[0]

    cond = jnp.broadcast_to(jnp.stack([c[0], c_ctx])[:, :, None], (2, D_MODEL, 128))
    mods = _ada_mod(cond, w_ada, b_ada)

    rope = _rope_tables(l)
    rope_ctx = _identity_rope_tables(lc)
    w_in_b = w_in.astype(BF16)
    w_o_b = w_o.astype(BF16)
    w_up_b = w_up.astype(BF16)
    w_down_b = w_down.astype(BF16)
    starts, pats, geoms = _na_plan(l)
    starts = jnp.asarray(starts)
    pats = jnp.asarray(pats)

    for layer in range(DEPTH):
        last = layer == DEPTH - 1
        lambda_init = 0.8 - 0.6 * math.exp(-0.3 * layer)
        sh_a, sc_a, g_a, sh_m, sc_m, g_m = jnp.split(mods[layer, 0:1], N_MOD, axis=-1)
        shc_a, scc_a, gc_a, shc_m, scc_m, gc_m = jnp.split(mods[layer, 1:2], N_MOD, axis=-1)
        lam_vec = da_lambda[layer]
        subln_g = da_subln[layer].reshape(1, 2 * HEAD_DIM)
        ln1 = (ln1_g[layer].reshape(1, D_MODEL), ln1_b[layer].reshape(1, D_MODEL))
        ln2 = (ln2_g[layer].reshape(1, D_MODEL), ln2_b[layer].reshape(1, D_MODEL))
        cw = conv_w[layer]
        cb = conv_b[layer].reshape(1, D_FF)

        qkv = _in_proj(xs, sc_a, sh_a, w_in_b, layer, *rope, tm=256)
        qkvc = _in_proj(xc, scc_a, shc_a, w_in_b, layer, *rope_ctx, tm=lc)

        od = _diff_attn(qkv, qkvc, lam_vec, subln_g, lambda_init, tq=512, tk=512)
        bias = _na_bias_table(na_rpb[layer], geoms)
        on = _na_attn(qkv, qkvc, bias, starts, pats)

        xs = _out_proj_ln(od, on, w_o_b, layer, xs, g_a, *ln1, tm=512)
        xs = _ffn_ln(xs, sc_m, sh_m, g_m, w_up_b, cw, cb, w_down_b, layer, *ln2, tm=512)

        if not last:
            odc, onc = _ctx_attn(qkvc, lam_vec, subln_g, lambda_init)
            xc = _out_proj_ln(odc, onc, w_o_b, layer, xc, gc_a, *ln1, tm=lc)
            xc = _ffn_ln(xc, scc_m, shc_m, gc_m, w_up_b, cw, cb, w_down_b, layer, *ln2, tm=lc)

    return xs[None]
```

```python
import functools
import math

import numpy as np
import jax
import jax.numpy as jnp
from jax import lax
from jax.experimental import pallas as pl
from jax.experimental.pallas import tpu as pltpu

D_MODEL = 2048
DEPTH = 2
GRID_W = 64
HEAD_DIM = 128
DA_HEADS = 4
NA_HEADS = 8
DA_WIDTH = DA_HEADS * 2 * HEAD_DIM
NA_WIDTH = NA_HEADS * HEAD_DIM
IN_DIM = 3 * DA_WIDTH + 3 * NA_WIDTH
NA_KH = 8
NA_KW = 16
D_FF = 5632
ROPE_THETA = 10000.0
LN_EPS = 1e-5
N_MOD = 6
ALPHA = (2.0 * DEPTH) ** 0.25
LOG2E = math.log2(math.e)
Q_SCALE = (HEAD_DIM ** -0.5) * LOG2E
MASK_VALUE = -1e30

F32 = jnp.float32
BF16 = jnp.bfloat16

VMEM_LIMIT = 56 * 1024 * 1024

QD_COL, KD_COL, VD_COL = 0, DA_WIDTH // 128, 2 * DA_WIDTH // 128
QN_COL = 3 * DA_WIDTH // 128
KN_COL = QN_COL + NA_WIDTH // 128
VN_COL = KN_COL + NA_WIDTH // 128


def _params(semantics):
    return pltpu.CompilerParams(dimension_semantics=semantics, vmem_limit_bytes=VMEM_LIMIT)


def _dot(a, b):
    return jnp.dot(a, b, preferred_element_type=F32)


def _dot_nt(a, b):
    return lax.dot_general(a, b, (((1,), (1,)), ((), ())), preferred_element_type=F32)


def _layernorm_rows(z, g, b):
    mu = jnp.mean(z, axis=-1, keepdims=True)
    zc = z - mu
    var = jnp.mean(zc * zc, axis=-1, keepdims=True)
    return zc * lax.rsqrt(var + LN_EPS) * g + b


ADA_TN = 1024
ADA_KC = 256


def _ada_kernel(c_ref, w_ref, b_ref, o_ref):
    reps = w_ref.shape[1] // 128
    for r in range(c_ref.shape[0]):
        acc = b_ref[...]
        for k0 in range(0, D_MODEL, ADA_KC):
            a = c_ref[r, k0:k0 + ADA_KC, :]
            silu = a / (1.0 + jnp.exp(-a))
            sw = jnp.concatenate([silu] * reps, axis=1)
            acc = acc + jnp.sum(w_ref[k0:k0 + ADA_KC, :] * sw, axis=0, keepdims=True)
        o_ref[r:r + 1, :] = acc


def _ada_mod(cond, w_ada, b_ada):
    nmod = N_MOD * D_MODEL
    nrow = cond.shape[0]
    return pl.pallas_call(
        _ada_kernel,
        grid=(DEPTH, nmod // ADA_TN),
        in_specs=[
            pl.BlockSpec((nrow, D_MODEL, 128), lambda l, j: (0, 0, 0)),
            pl.BlockSpec((None, D_MODEL, ADA_TN), lambda l, j: (l, 0, j)),
            pl.BlockSpec((None, 1, ADA_TN), lambda l, j: (l, 0, j)),
        ],
        out_specs=pl.BlockSpec((None, nrow, ADA_TN), lambda l, j: (l, 0, j)),
        out_shape=jax.ShapeDtypeStruct((DEPTH, nrow, nmod), F32),
        compiler_params=_params(("parallel", "parallel")),
        name="ada_mod",
    )(cond, w_ada, b_ada.reshape(DEPTH, 1, nmod))


PROJ_TN = 1024


def _rope_tile(row_ref, col_ref, i, tm):
    rows_per_tile = tm // GRID_W
    parts = [jnp.broadcast_to(row_ref[pl.ds(i * rows_per_tile + k, 1), :], (GRID_W, HEAD_DIM)) + col_ref[...]
             for k in range(rows_per_tile)]
    return jnp.concatenate(parts, axis=0)


def _in_proj_kernel(x_ref, sc_ref, sh_ref, w_ref, cosr_ref, sar_ref, sbr_ref, cosc_ref, sac_ref, sbc_ref,
                    o_ref, h_ref):
    i = pl.program_id(0)
    tm = x_ref.shape[0]
    h_ref[...] = (x_ref[...] * (1.0 + sc_ref[...]) + sh_ref[...]).astype(BF16)
    cos_t = _rope_tile(cosr_ref, cosc_ref, i, tm)
    sa_t = _rope_tile(sar_ref, sac_ref, i, tm)
    sb_t = _rope_tile(sbr_ref, sbc_ref, i, tm)
    for j in range(IN_DIM // PROJ_TN):
        acc = _dot(h_ref[...], w_ref[:, j * PROJ_TN:(j + 1) * PROJ_TN])
        if j <= 1:
            scale = Q_SCALE if j == 0 else 1.0
            cos = cos_t * scale
            sa = sa_t * scale
            sb = sb_t * scale
            for k in range(PROJ_TN // HEAD_DIM):
                a = acc[:, k * HEAD_DIM:(k + 1) * HEAD_DIM]
                r = a * cos + pltpu.roll(a, 96, 1) * sa + pltpu.roll(a, 32, 1) * sb
                o_ref[:, j * PROJ_TN + k * HEAD_DIM:j * PROJ_TN + (k + 1) * HEAD_DIM] = r.astype(BF16)
        elif j == 3:
            o_ref[:, j * PROJ_TN:(j + 1) * PROJ_TN] = (acc * Q_SCALE).astype(BF16)
        else:
            o_ref[:, j * PROJ_TN:(j + 1) * PROJ_TN] = acc.astype(BF16)


def _in_proj(x, sc, sh, w, layer, cos_r, sa_r, sb_r, cos_c, sa_c, sb_c, tm):
    m = x.shape[0]
    nrow = cos_r.shape[0]
    tab_r = pl.BlockSpec((nrow, HEAD_DIM), lambda i: (0, 0))
    tab_c = pl.BlockSpec((GRID_W, HEAD_DIM), lambda i: (0, 0))
    return pl.pallas_call(
        _in_proj_kernel,
        grid=(m // tm,),
        in_specs=[
            pl.BlockSpec((tm, D_MODEL), lambda i: (i, 0)),
            pl.BlockSpec((1, D_MODEL), lambda i: (0, 0)),
            pl.BlockSpec((1, D_MODEL), lambda i: (0, 0)),
            pl.BlockSpec((None, D_MODEL, IN_DIM), lambda i: (layer, 0, 0), pipeline_mode=pl.Buffered(1)),
            tab_r, tab_r, tab_r, tab_c, tab_c, tab_c,
        ],
        out_specs=pl.BlockSpec((tm, IN_DIM), lambda i: (i, 0)),
        out_shape=jax.ShapeDtypeStruct((m, IN_DIM), BF16),
        scratch_shapes=[pltpu.VMEM((tm, D_MODEL), BF16)],
        compiler_params=_params(("parallel",)),
        name="in_proj",
    )(x, sc, sh, w, cos_r, sa_r, sb_r, cos_c, sa_c, sb_c)


UNROLL = 8


def _lambda_value(lv, lambda_init):
    a = jnp.sum(lv[0:1, :] * lv[1:2, :], axis=-1, keepdims=True)
    b = jnp.sum(lv[2:3, :] * lv[3:4, :], axis=-1, keepdims=True)
    return jnp.exp(a) - jnp.exp(b) + lambda_init


def _subln(o, g, lambda_init):
    ms = jnp.mean(o * o, axis=-1, keepdims=True)
    return o * lax.rsqrt(ms + LN_EPS) * g * (1.0 - lambda_init)


def _diff_attn_kernel(q_ref, qn_ref, k_ref, v_ref, kc_ref, vc_ref, lam_ref, g_ref, o_ref,
                      vt_ref, qt_ref, mx_ref, acc1_ref, acc2_ref, sa1_ref, sa2_ref, sb1_ref, sb2_ref,
                      *, tk, lambda_init):
    nk = k_ref.shape[0] // tk
    accs = (acc1_ref, acc2_ref)
    bufs = ((sa1_ref, sa2_ref), (sb1_ref, sb2_ref))
    first = pl.program_id(1) == 0
    unroll = math.gcd(nk, UNROLL)
    assert unroll % 2 == 0
    trips = nk // unroll

    def scores_into(i, dst, slot):
        off = pl.multiple_of(i * tk, tk)
        mx = []
        for c in range(2):
            cols = slice(c * HEAD_DIM, (c + 1) * HEAD_DIM)
            s = _dot(k_ref[pl.ds(off, tk), cols], qt_ref[slot, cols, :])
            dst[c][...] = s
            mx.append(jnp.max(s, axis=0, keepdims=True))
        return tuple(mx)

    @pl.when(first)
    def _():
        def fill(i, carry):
            off = pl.multiple_of(i * tk, tk)
            vt_ref[i] = v_ref[pl.ds(off, tk), :].astype(F32).T.astype(BF16)
            return carry
        lax.fori_loop(0, nk, fill, 0)
        qt_ref[0] = q_ref[...].astype(F32).T.astype(BF16)
        mx0 = scores_into(0, bufs[0], 0)
        mx_ref[0:1, :] = mx0[0]
        mx_ref[1:2, :] = mx0[1]

    @pl.when(jnp.logical_not(first))
    def _():
        qt_ref[0] = qt_ref[1]

    def softmax_pv(i, src, mx, carry):
        vt = vt_ref[i]
        out = []
        for c in range(2):
            if carry is None:
                mn = mx[c]
                p = jnp.exp2(src[c][...] - mn)
                l = jnp.sum(p, axis=0, keepdims=True)
                accs[c][...] = _dot(vt, p.astype(BF16))
            else:
                m, l = carry[2 * c], carry[2 * c + 1]
                mn = jnp.maximum(m, mx[c])
                a = jnp.exp2(m - mn)
                p = jnp.exp2(src[c][...] - mn)
                l = a * l + jnp.sum(p, axis=0, keepdims=True)
                accs[c][...] = a * accs[c][...] + _dot(vt, p.astype(BF16))
            out += [mn, l]
        return tuple(out)

    def context_step(carry):
        vc_t = vc_ref[...].astype(F32).T.astype(BF16)
        out = []
        for c in range(2):
            cols = slice(c * HEAD_DIM, (c + 1) * HEAD_DIM)
            m, l = carry[2 * c], carry[2 * c + 1]
            s = _dot(kc_ref[:, cols], qt_ref[0, cols, :])
            mn = jnp.maximum(m, jnp.max(s, axis=0, keepdims=True))
            a = jnp.exp2(m - mn)
            p = jnp.exp2(s - mn)
            l = a * l + jnp.sum(p, axis=0, keepdims=True)
            accs[c][...] = a * accs[c][...] + _dot(vc_t, p.astype(BF16))
            out += [mn, l]
        return tuple(out)

    def blocks(trip, mx, carry):
        base = trip * unroll
        for u in range(unroll):
            if u == unroll - 1:
                wrap = trip == trips - 1
                nxt = jnp.where(wrap, 0, base + unroll)
                mx_next = scores_into(nxt, bufs[(u + 1) % 2], jnp.where(wrap, 1, 0))
            else:
                mx_next = scores_into(base + u + 1, bufs[(u + 1) % 2], 0)
            start = carry is None
            carry = softmax_pv(base + u, bufs[u % 2], mx, carry)
            if start:
                carry = context_step(carry)
                qt_ref[1] = qn_ref[...].astype(F32).T.astype(BF16)
            mx = mx_next
        return mx, carry

    state = blocks(0, (mx_ref[0:1, :], mx_ref[1:2, :]), None)

    mx_n, (m1, l1, m2, l2) = lax.fori_loop(1, trips, lambda j, st: blocks(j, st[0], st[1]), state)
    mx_ref[0:1, :] = mx_n[0]
    mx_ref[1:2, :] = mx_n[1]

    lam = _lambda_value(lam_ref[...], lambda_init)
    o_t = acc1_ref[...] / l1 - lam * (acc2_ref[...] / l2)
    ms = jnp.mean(o_t * o_t, axis=0, keepdims=True)
    o = (o_t * lax.rsqrt(ms + LN_EPS)).T
    o_ref[...] = (o * g_ref[...] * (1.0 - lambda_init)).astype(BF16)


def _diff_attn(qkv, qkvc, lam_vec, subln_g, lambda_init, tq, tk):
    l = qkv.shape[0]
    lc = qkvc.shape[0]
    kern = functools.partial(_diff_attn_kernel, tk=tk, lambda_init=lambda_init)
    return pl.pallas_call(
        kern,
        grid=(DA_HEADS, l // tq),
        in_specs=[
            pl.BlockSpec((tq, 256), lambda h, i: (i, QD_COL // 2 + h)),
            pl.BlockSpec((tq, 256), lambda h, i: (jnp.minimum(i + 1, l // tq - 1), QD_COL // 2 + h)),
            pl.BlockSpec((l, 256), lambda h, i: (0, KD_COL // 2 + h)),
            pl.BlockSpec((l, 256), lambda h, i: (0, VD_COL // 2 + h)),
            pl.BlockSpec((lc, 256), lambda h, i: (0, KD_COL // 2 + h)),
            pl.BlockSpec((lc, 256), lambda h, i: (0, VD_COL // 2 + h)),
            pl.BlockSpec((4, HEAD_DIM), lambda h, i: (0, 0)),
            pl.BlockSpec((1, 2 * HEAD_DIM), lambda h, i: (0, 0)),
        ],
        out_specs=pl.BlockSpec((tq, 256), lambda h, i: (i, h)),
        out_shape=jax.ShapeDtypeStruct((l, DA_WIDTH), BF16),
        scratch_shapes=[pltpu.VMEM((l // tk, 256, tk), BF16),
                        pltpu.VMEM((2, 256, tq), BF16), pltpu.VMEM((8, tq), F32),
                        pltpu.VMEM((256, tq), F32), pltpu.VMEM((256, tq), F32)]
        + [pltpu.VMEM((tk, tq), F32)] * 4,
        compiler_params=_params(("parallel", "arbitrary")),
        name="diff_attn",
    )(qkv, qkv, qkv, qkv, qkvc, qkvc, lam_vec, subln_g)


NA_ROWS = 4
NA_TQ = NA_ROWS * GRID_W
NA_SLAB = NA_ROWS + NA_KH - 1
NA_SK = NA_SLAB * GRID_W


def _na_plan(l):
    rows = l // GRID_W
    kh = min(NA_KH, rows)
    assert kh == NA_KH and rows >= NA_SLAB and rows % NA_ROWS == 0
    nblk = rows // NA_ROWS
    starts, pats, geoms = [], [], []
    for b in range(nblk):
        r0 = b * NA_ROWS
        kstart = int(np.clip(r0 - kh // 2, 0, rows - NA_SLAB))
        rs = [int(np.clip(r0 + qr - kh // 2, 0, rows - kh)) for qr in range(NA_ROWS)]
        geom = (r0 - kstart, tuple(x - kstart for x in rs))
        assert all(0 <= x and x + kh <= NA_SLAB for x in geom[1])
        if geom not in geoms:
            geoms.append(geom)
        starts.append(kstart * GRID_W)
        pats.append(geoms.index(geom))
    return np.asarray(starts, np.int32), np.asarray(pats, np.int32), geoms


def _na_bias_kernel(t2_ref, o_ref, *, geoms):
    p = pl.program_id(1)
    lane = lax.broadcasted_iota(jnp.int32, (GRID_W, 2 * GRID_W), 1)
    for ps, (dq, rs) in enumerate(geoms):
        @pl.when(p == ps)
        def _(dq=dq, rs=rs):
            for qr in range(NA_ROWS):
                def inside(j, qr=qr):
                    return j < NA_SLAB and rs[qr] <= j < rs[qr] + NA_KH
                for j0 in range(0, NA_SLAB, 2):
                    left, right = inside(j0), inside(j0 + 1)
                    d0 = j0 - (dq + qr) + (NA_KH - 1)
                    if left or right:
                        blk = t2_ref[d0 + 1]
                        if not left:
                            blk = jnp.where(lane < GRID_W, MASK_VALUE, blk)
                        if not right:
                            blk = jnp.where(lane < GRID_W, blk, MASK_VALUE)
                    else:
                        blk = jnp.full((GRID_W, 2 * GRID_W), MASK_VALUE, F32)
                    width = min(2, NA_SLAB - j0) * GRID_W
                    o_ref[qr * GRID_W:(qr + 1) * GRID_W, j0 * GRID_W:j0 * GRID_W + width] = blk[:, :width]


def _na_bias_table(rpb, geoms):
    c = np.arange(GRID_W)
    dc = c[None, :] - c[:, None] + (NA_KW - 1)
    cs = np.clip(c - NA_KW // 2, 0, GRID_W - NA_KW)
    col_ok = (c[None, :] >= cs[:, None]) & (c[None, :] < cs[:, None] + NA_KW)
    e_col = np.zeros((2 * NA_KW - 1, GRID_W, GRID_W), np.float32)
    for ci in range(GRID_W):
        for ki in range(GRID_W):
            if col_ok[ci, ki]:
                e_col[dc[ci, ki], ci, ki] = 1.0
    t = jnp.einsum('hrd,dck->hrck', rpb.astype(F32), jnp.asarray(e_col), precision=lax.Precision.HIGHEST)
    t = jnp.where(jnp.asarray(col_ok)[None, None], t * LOG2E, MASK_VALUE)
    tp = jnp.pad(t, ((0, 0), (1, 1), (0, 0), (0, 0)))
    t2 = jnp.concatenate([tp[:, :-1], tp[:, 1:]], axis=-1)
    npat = len(geoms)
    return pl.pallas_call(
        functools.partial(_na_bias_kernel, geoms=tuple(geoms)),
        grid=(NA_HEADS, npat),
        in_specs=[pl.BlockSpec((None, 2 * NA_KH, GRID_W, 2 * GRID_W), lambda h, p: (h, 0, 0, 0))],
        out_specs=pl.BlockSpec((None, None, NA_TQ, NA_SK), lambda h, p: (p, h, 0, 0)),
        out_shape=jax.ShapeDtypeStruct((npat, NA_HEADS, NA_TQ, NA_SK), F32),
        compiler_params=_params(("parallel", "arbitrary")),
        name="na_bias",
    )(t2)


NA_HPS = 4
NA_QB = 2


def _na_attn_kernel(start_ref, pat_ref, q_ref, k_ref, v_ref, kc_ref, vc_ref, *rest):
    del pat_ref
    bias_refs, o_ref = rest[:NA_QB], rest[NA_QB]
    i = pl.program_id(1)
    for b in range(NA_QB):
        off = pl.multiple_of(start_ref[NA_QB * i + b], GRID_W)
        rows = slice(b * NA_TQ, (b + 1) * NA_TQ)
        for h in range(NA_HPS):
            cols = slice(h * HEAD_DIM, (h + 1) * HEAD_DIM)
            q = q_ref[rows, cols]
            s_loc = _dot_nt(q, k_ref[pl.ds(off, NA_SK), cols]) + bias_refs[b][h]
            s_ctx = _dot_nt(q, kc_ref[:, cols])
            m = jnp.maximum(jnp.max(s_loc, axis=-1, keepdims=True), jnp.max(s_ctx, axis=-1, keepdims=True))
            p_loc = jnp.exp2(s_loc - m)
            p_ctx = jnp.exp2(s_ctx - m)
            l = jnp.sum(p_loc, axis=-1, keepdims=True) + jnp.sum(p_ctx, axis=-1, keepdims=True)
            o = (_dot(p_loc.astype(BF16), v_ref[pl.ds(off, NA_SK), cols])
                 + _dot(p_ctx.astype(BF16), vc_ref[:, cols]))
            o_ref[rows, cols] = (o / l).astype(BF16)


def _na_attn(qkv, qkvc, bias, starts, pats):
    l = qkv.shape[0]
    lc = qkvc.shape[0]
    w = NA_HPS * HEAD_DIM
    tq = NA_QB * NA_TQ
    assert l % tq == 0
    once = pl.Buffered(1)

    def bias_spec(b):
        return pl.BlockSpec((None, NA_HPS, NA_TQ, NA_SK), lambda h, i, st, pt: (pt[NA_QB * i + b], h, 0, 0))

    grid_spec = pltpu.PrefetchScalarGridSpec(
        num_scalar_prefetch=2,
        grid=(NA_HEADS // NA_HPS, l // tq),
        in_specs=[
            pl.BlockSpec((tq, w), lambda h, i, st, pt: (i, QN_COL // NA_HPS + h)),
            pl.BlockSpec((l, w), lambda h, i, st, pt: (0, KN_COL // NA_HPS + h), pipeline_mode=once),
            pl.BlockSpec((l, w), lambda h, i, st, pt: (0, VN_COL // NA_HPS + h), pipeline_mode=once),
            pl.BlockSpec((lc, w), lambda h, i, st, pt: (0, KN_COL // NA_HPS + h)),
            pl.BlockSpec((lc, w), lambda h, i, st, pt: (0, VN_COL // NA_HPS + h)),
        ] + [bias_spec(b) for b in range(NA_QB)],
        out_specs=pl.BlockSpec((tq, w), lambda h, i, st, pt: (i, h)),
    )
    return pl.pallas_call(
        _na_attn_kernel,
        grid_spec=grid_spec,
        out_shape=jax.ShapeDtypeStruct((l, NA_WIDTH), BF16),
        compiler_params=_params(("parallel", "arbitrary")),
        name="na_attn",
    )(starts, pats, qkv, qkv, qkv, qkvc, qkvc, *([bias] * NA_QB))


def _softmax_pv(q, k, v):
    s = _dot_nt(q, k)
    m = jnp.max(s, axis=-1, keepdims=True)
    p = jnp.exp2(s - m)
    l = jnp.sum(p, axis=-1, keepdims=True)
    return _dot(p.astype(BF16), v) / l


def _ctx_attn_kernel(x_ref, lam_ref, g_ref, od_ref, on_ref, *, lambda_init):
    lam = _lambda_value(lam_ref[...], lambda_init)
    g = g_ref[...]

    def col(base, h, width=HEAD_DIM):
        lo = (base + h) * HEAD_DIM
        return x_ref[:, lo:lo + width]

    for h in range(DA_HEADS):
        v = col(VD_COL, 2 * h, 2 * HEAD_DIM)
        o1 = _softmax_pv(col(QD_COL, 2 * h), col(KD_COL, 2 * h), v)
        o2 = _softmax_pv(col(QD_COL, 2 * h + 1), col(KD_COL, 2 * h + 1), v)
        od_ref[:, h * 256:(h + 1) * 256] = _subln(o1 - lam * o2, g, lambda_init).astype(BF16)
    for h in range(NA_HEADS):
        o = _softmax_pv(col(QN_COL, h), col(KN_COL, h), col(VN_COL, h))
        on_ref[:, h * HEAD_DIM:(h + 1) * HEAD_DIM] = o.astype(BF16)


def _ctx_attn(qkvc, lam_vec, subln_g, lambda_init):
    lc = qkvc.shape[0]
    kern = functools.partial(_ctx_attn_kernel, lambda_init=lambda_init)
    return pl.pallas_call(
        kern,
        grid=(1,),
        in_specs=[
            pl.BlockSpec((lc, IN_DIM), lambda i: (0, 0)),
            pl.BlockSpec((4, HEAD_DIM), lambda i: (0, 0)),
            pl.BlockSpec((1, 2 * HEAD_DIM), lambda i: (0, 0)),
        ],
        out_specs=[pl.BlockSpec((lc, DA_WIDTH), lambda i: (0, 0)),
                   pl.BlockSpec((lc, NA_WIDTH), lambda i: (0, 0))],
        out_shape=[jax.ShapeDtypeStruct((lc, DA_WIDTH), BF16),
                   jax.ShapeDtypeStruct((lc, NA_WIDTH), BF16)],
        compiler_params=_params(("arbitrary",)),
        name="ctx_attn",
    )(qkvc, lam_vec, subln_g)


OUT_SUB = 256


def _out_proj_kernel(od_ref, on_ref, w_ref, x_ref, gate_ref, g_ref, b_ref, o_ref):
    tm = x_ref.shape[0]
    for r0 in range(0, tm, OUT_SUB):
        rows = slice(r0, r0 + OUT_SUB)
        y = _dot(od_ref[rows, :], w_ref[0:DA_WIDTH, :]) + _dot(on_ref[rows, :], w_ref[DA_WIDTH:, :])
        z = ALPHA * x_ref[rows, :] + gate_ref[...] * y
        o_ref[rows, :] = _layernorm_rows(z, g_ref[...], b_ref[...])


def _out_proj_ln(od, on, w, layer, x, gate, ln_g, ln_b, tm):
    m = x.shape[0]
    vec = pl.BlockSpec((1, D_MODEL), lambda i: (0, 0))
    return pl.pallas_call(
        _out_proj_kernel,
        grid=(m // tm,),
        in_specs=[
            pl.BlockSpec((tm, DA_WIDTH), lambda i: (i, 0)),
            pl.BlockSpec((tm, NA_WIDTH), lambda i: (i, 0)),
            pl.BlockSpec((None, DA_WIDTH + NA_WIDTH, D_MODEL), lambda i: (layer, 0, 0),
                         pipeline_mode=pl.Buffered(1)),
            pl.BlockSpec((tm, D_MODEL), lambda i: (i, 0)),
            vec, vec, vec,
        ],
        out_specs=pl.BlockSpec((tm, D_MODEL), lambda i: (i, 0)),
        out_shape=jax.ShapeDtypeStruct((m, D_MODEL), F32),
        compiler_params=_params(("parallel",)),
        name="out_proj_ln",
    )(od, on, w, x, gate, ln_g, ln_b)


FFN_TF = 512
HALO = 16


def _ffn_kernel(x_ref, xp_ref, xn_ref, sc_ref, sh_ref, gate_ref, wg_ref, wu_ref, cw_ref, cb_ref, wd_ref,
                g_ref, b_ref, o_ref, h_ref, acc_ref):
    i = pl.program_id(0)
    c = pl.program_id(1)
    tm = x_ref.shape[0]
    n = tm + HALO

    @pl.when(c == 0)
    def _():
        sc = 1.0 + sc_ref[...]
        sh = sh_ref[...]
        before = jnp.where(i == 0, 0.0, xp_ref[HALO - 1:HALO, :] * sc + sh)
        after = jnp.where(i == pl.num_programs(0) - 1, 0.0, xn_ref[0:1, :] * sc + sh)
        row = lax.broadcasted_iota(jnp.int32, (HALO, D_MODEL), 0)
        tail = jnp.where(row == 0, after, jnp.where(row == HALO - 1, before, 0.0))
        h_ref[0:tm, :] = (x_ref[...] * sc + sh).astype(BF16)
        h_ref[tm:n, :] = tail.astype(BF16)
        acc_ref[...] = jnp.zeros_like(acc_ref)

    g_ext = _dot(h_ref[...], wg_ref[...])
    u = _dot(h_ref[0:tm, :], wu_ref[...])
    g = g_ext[0:tm, :]
    g_prev = pltpu.roll(g_ext, 1, 0)[0:tm, :]
    g_next = pltpu.roll(g_ext, n - 1, 0)[0:tm, :]
    gc = g_prev * cw_ref[0:1, :] + g * cw_ref[1:2, :] + g_next * cw_ref[2:3, :] + cb_ref[...]
    act = gc / (1.0 + jnp.exp(-gc)) * u
    acc_ref[...] += _dot(act.astype(BF16), wd_ref[...])

    @pl.when(c == pl.num_programs(1) - 1)
    def _():
        z = ALPHA * x_ref[...] + gate_ref[...] * acc_ref[...]
        o_ref[...] = _layernorm_rows(z, g_ref[...], b_ref[...])


def _ffn_ln(x, sc, sh, gate, w_up, conv_w, conv_b, w_down, layer, ln_g, ln_b, tm):
    m = x.shape[0]
    nf = D_FF // FFN_TF
    tb = tm // HALO
    last_blk = m // HALO - 1
    vec = pl.BlockSpec((1, D_MODEL), lambda i, c: (0, 0))
    return pl.pallas_call(
        _ffn_kernel,
        grid=(m // tm, nf),
        in_specs=[
            pl.BlockSpec((tm, D_MODEL), lambda i, c: (i, 0)),
            pl.BlockSpec((HALO, D_MODEL), lambda i, c: (jnp.maximum(i * tb - 1, 0), 0)),
            pl.BlockSpec((HALO, D_MODEL), lambda i, c: (jnp.minimum((i + 1) * tb, last_blk), 0)),
            vec, vec, vec,
            pl.BlockSpec((None, D_MODEL, FFN_TF), lambda i, c: (layer, 0, c)),
            pl.BlockSpec((None, D_MODEL, FFN_TF), lambda i, c: (layer, 0, nf + c)),
            pl.BlockSpec((3, FFN_TF), lambda i, c: (0, c)),
            pl.BlockSpec((1, FFN_TF), lambda i, c: (0, c)),
            pl.BlockSpec((None, FFN_TF, D_MODEL), lambda i, c: (layer, c, 0)),
            vec, vec,
        ],
        out_specs=pl.BlockSpec((tm, D_MODEL), lambda i, c: (i, 0)),
        out_shape=jax.ShapeDtypeStruct((m, D_MODEL), F32),
        scratch_shapes=[pltpu.VMEM((tm + HALO, D_MODEL), BF16),
                        pltpu.VMEM((tm, D_MODEL), F32)],
        compiler_params=_params(("parallel", "arbitrary")),
        name="ffn_ln",
    )(x, x, x, sc, sh, gate, w_up, w_up, conv_w, conv_b, w_down, ln_g, ln_b)


def _rope_tables(l):
    half = HEAD_DIM // 2
    inv_freq = 1.0 / (ROPE_THETA ** (jnp.arange(0, half, 2, dtype=F32) / half))
    zeros = jnp.zeros((1, half), F32)

    def tables(pos, row_part):
        ang = pos.astype(F32)[:, None] * inv_freq[None, :]
        ang = jnp.concatenate([ang, ang], axis=-1)
        cos, sin = jnp.cos(ang), jnp.sin(ang)
        first = jnp.asarray(np.arange(half) < half // 2)[None, :]
        sin_a = jnp.where(first, -sin, 0.0)
        sin_b = jnp.where(first, 0.0, sin)
        pad = jnp.broadcast_to(zeros, cos.shape)
        order = (lambda t: jnp.concatenate([t, pad], axis=-1)) if row_part else \
                (lambda t: jnp.concatenate([pad, t], axis=-1))
        return order(cos), order(sin_a), order(sin_b)

    rows = tables(jnp.arange(l // GRID_W, dtype=jnp.int32), True)
    cols = tables(jnp.arange(GRID_W, dtype=jnp.int32), False)
    return rows + cols


def _identity_rope_tables(lc):
    nrow = lc // GRID_W
    lanes = jnp.asarray(np.arange(HEAD_DIM) < HEAD_DIM // 2, F32)[None, :]
    zr = jnp.zeros((nrow, HEAD_DIM), F32)
    zc = jnp.zeros((GRID_W, HEAD_DIM), F32)
    return (jnp.broadcast_to(lanes, (nrow, HEAD_DIM)), zr, zr,
            jnp.broadcast_to(1.0 - lanes, (GRID_W, HEAD_DIM)), zc, zc)


def kernel(x, c, ctx, c_ctx, w_ada, b_ada, w_in, da_lambda, da_subln, na_rpb, w_o,
           ln1_g, ln1_b, w_up, conv_w, conv_b, w_down, ln2_g, ln2_b):
    assert x.shape[0] == 1 and ctx.shape[0] == 1
    l = x.shape[1]
    lc = ctx.shape[1]
    xs = x[0]
    xc = ctx[0]

    cond = jnp.broadcast_to(jnp.stack([c[0], c_ctx])[:, :, None], (2, D_MODEL, 128))
    mods = _ada_mod(cond, w_ada, b_ada)

    rope = _rope_tables(l)
    rope_ctx = _identity_rope_tables(lc)
    w_in_b = w_in.astype(BF16)
    w_o_b = w_o.astype(BF16)
    w_up_b = w_up.astype(BF16)
    w_down_b = w_down.astype(BF16)
    starts, pats, geoms = _na_plan(l)
    starts = jnp.asarray(starts)
    pats = jnp.asarray(pats)

    for layer in range(DEPTH):
        last = layer == DEPTH - 1
        lambda_init = 0.8 - 0.6 * math.exp(-0.3 * layer)
        sh_a, sc_a, g_a, sh_m, sc_m, g_m = jnp.split(mods[layer, 0:1], N_MOD, axis=-1)
        shc_a, scc_a, gc_a, shc_m, scc_m, gc_m = jnp.split(mods[layer, 1:2], N_MOD, axis=-1)
        lam_vec = da_lambda[layer]
        subln_g = da_subln[layer].reshape(1, 2 * HEAD_DIM)
        ln1 = (ln1_g[layer].reshape(1, D_MODEL), ln1_b[layer].reshape(1, D_MODEL))
        ln2 = (ln2_g[layer].reshape(1, D_MODEL), ln2_b[layer].reshape(1, D_MODEL))
        cw = conv_w[layer]
        cb = conv_b[layer].reshape(1, D_FF)

        qkv = _in_proj(xs, sc_a, sh_a, w_in_b, layer, *rope, tm=256)
        qkvc = _in_proj(xc, scc_a, shc_a, w_in_b, layer, *rope_ctx, tm=lc)

        od = _diff_attn(qkv, qkvc, lam_vec, subln_g, lambda_init, tq=512, tk=512)
        bias = _na_bias_table(na_rpb[layer], geoms)
        on = _na_attn(qkv, qkvc, bias, starts, pats)

        xs = _out_proj_ln(od, on, w_o_b, layer, xs, g_a, *ln1, tm=512)
        xs = _ffn_ln(xs, sc_m, sh_m, g_m, w_up_b, cw, cb, w_down_b, layer, *ln2, tm=512)

        if not last:
            odc, onc = _ctx_attn(qkvc, lam_vec, subln_g, lambda_init)
            xc = _out_proj_ln(odc, onc, w_o_b, layer, xc, gc_a, *ln1, tm=lc)
            xc = _ffn_ln(xc, scc_m, shc_m, gc_m, w_up_b, cw, cb, w_down_b, layer, *ln2, tm=lc)

    return xs[None]
```

```python
import functools
import math

import numpy as np
import jax
import jax.numpy as jnp
from jax import lax
from jax.experimental import pallas as pl
from jax.experimental.pallas import tpu as pltpu

D_MODEL = 2048
DEPTH = 2
GRID_W = 64
HEAD_DIM = 128
DA_HEADS = 4
NA_HEADS = 8
DA_WIDTH = DA_HEADS * 2 * HEAD_DIM
NA_WIDTH = NA_HEADS * HEAD_DIM
IN_DIM = 3 * DA_WIDTH + 3 * NA_WIDTH
NA_KH = 8
NA_KW = 16
D_FF = 5632
ROPE_THETA = 10000.0
LN_EPS = 1e-5
N_MOD = 6
ALPHA = (2.0 * DEPTH) ** 0.25
LOG2E = math.log2(math.e)
Q_SCALE = (HEAD_DIM ** -0.5) * LOG2E
MASK_VALUE = -1e30

F32 = jnp.float32
BF16 = jnp.bfloat16

VMEM_LIMIT = 56 * 1024 * 1024

QD_COL, KD_COL, VD_COL = 0, DA_WIDTH // 128, 2 * DA_WIDTH // 128
QN_COL = 3 * DA_WIDTH // 128
KN_COL = QN_COL + NA_WIDTH // 128
VN_COL = KN_COL + NA_WIDTH // 128


def _params(semantics):
    return pltpu.CompilerParams(dimension_semantics=semantics, vmem_limit_bytes=VMEM_LIMIT)


def _dot(a, b):
    return jnp.dot(a, b, preferred_element_type=F32)


def _dot_nt(a, b):
    return lax.dot_general(a, b, (((1,), (1,)), ((), ())), preferred_element_type=F32)


def _layernorm_rows(z, g, b):
    mu = jnp.mean(z, axis=-1, keepdims=True)
    zc = z - mu
    var = jnp.mean(zc * zc, axis=-1, keepdims=True)
    return zc * lax.rsqrt(var + LN_EPS) * g + b


ADA_TN = 1024
ADA_KC = 256


def _ada_kernel(c_ref, w_ref, b_ref, o_ref):
    reps = w_ref.shape[1] // 128
    for r in range(c_ref.shape[0]):
        acc = b_ref[...]
        for k0 in range(0, D_MODEL, ADA_KC):
            a = c_ref[r, k0:k0 + ADA_KC, :]
            silu = a / (1.0 + jnp.exp(-a))
            sw = jnp.concatenate([silu] * reps, axis=1)
            acc = acc + jnp.sum(w_ref[k0:k0 + ADA_KC, :] * sw, axis=0, keepdims=True)
        o_ref[r:r + 1, :] = acc


def _ada_mod(cond, w_ada, b_ada):
    nmod = N_MOD * D_MODEL
    nrow = cond.shape[0]
    return pl.pallas_call(
        _ada_kernel,
        grid=(DEPTH, nmod // ADA_TN),
        in_specs=[
            pl.BlockSpec((nrow, D_MODEL, 128), lambda l, j: (0, 0, 0)),
            pl.BlockSpec((None, D_MODEL, ADA_TN), lambda l, j: (l, 0, j)),
            pl.BlockSpec((None, 1, ADA_TN), lambda l, j: (l, 0, j)),
        ],
        out_specs=pl.BlockSpec((None, nrow, ADA_TN), lambda l, j: (l, 0, j)),
        out_shape=jax.ShapeDtypeStruct((DEPTH, nrow, nmod), F32),
        compiler_params=_params(("parallel", "parallel")),
        name="ada_mod",
    )(cond, w_ada, b_ada.reshape(DEPTH, 1, nmod))


PROJ_TN = 1024


def _rope_tile(row_ref, col_ref, i, tm):
    rows_per_tile = tm // GRID_W
    parts = [jnp.broadcast_to(row_ref[pl.ds(i * rows_per_tile + k, 1), :], (GRID_W, HEAD_DIM)) + col_ref[...]
             for k in range(rows_per_tile)]
    return jnp.concatenate(parts, axis=0)


def _in_proj_kernel(x_ref, sc_ref, sh_ref, w_ref, cosr_ref, sar_ref, sbr_ref, cosc_ref, sac_ref, sbc_ref,
                    o_ref, h_ref):
    i = pl.program_id(0)
    tm = x_ref.shape[0]
    h_ref[...] = (x_ref[...] * (1.0 + sc_ref[...]) + sh_ref[...]).astype(BF16)
    cos_t = _rope_tile(cosr_ref, cosc_ref, i, tm)
    sa_t = _rope_tile(sar_ref, sac_ref, i, tm)
    sb_t = _rope_tile(sbr_ref, sbc_ref, i, tm)
    for j in range(IN_DIM // PROJ_TN):
        acc = _dot(h_ref[...], w_ref[:, j * PROJ_TN:(j + 1) * PROJ_TN])
        if j <= 1:
            scale = Q_SCALE if j == 0 else 1.0
            cos = cos_t * scale
            sa = sa_t * scale
            sb = sb_t * scale
            for k in range(PROJ_TN // HEAD_DIM):
                a = acc[:, k * HEAD_DIM:(k + 1) * HEAD_DIM]
                r = a * cos + pltpu.roll(a, 96, 1) * sa + pltpu.roll(a, 32, 1) * sb
                o_ref[:, j * PROJ_TN + k * HEAD_DIM:j * PROJ_TN + (k + 1) * HEAD_DIM] = r.astype(BF16)
        elif j == 3:
            o_ref[:, j * PROJ_TN:(j + 1) * PROJ_TN] = (acc * Q_SCALE).astype(BF16)
        else:
            o_ref[:, j * PROJ_TN:(j + 1) * PROJ_TN] = acc.astype(BF16)


def _in_proj(x, sc, sh, w, layer, cos_r, sa_r, sb_r, cos_c, sa_c, sb_c, tm):
    m = x.shape[0]
    nrow = cos_r.shape[0]
    tab_r = pl.BlockSpec((nrow, HEAD_DIM), lambda i: (0, 0))
    tab_c = pl.BlockSpec((GRID_W, HEAD_DIM), lambda i: (0, 0))
    return pl.pallas_call(
        _in_proj_kernel,
        grid=(m // tm,),
        in_specs=[
            pl.BlockSpec((tm, D_MODEL), lambda i: (i, 0)),
            pl.BlockSpec((1, D_MODEL), lambda i: (0, 0)),
            pl.BlockSpec((1, D_MODEL), lambda i: (0, 0)),
            pl.BlockSpec((None, D_MODEL, IN_DIM), lambda i: (layer, 0, 0), pipeline_mode=pl.Buffered(1)),
            tab_r, tab_r, tab_r, tab_c, tab_c, tab_c,
        ],
        out_specs=pl.BlockSpec((tm, IN_DIM), lambda i: (i, 0)),
        out_shape=jax.ShapeDtypeStruct((m, IN_DIM), BF16),
        scratch_shapes=[pltpu.VMEM((tm, D_MODEL), BF16)],
        compiler_params=_params(("parallel",)),
        name="in_proj",
    )(x, sc, sh, w, cos_r, sa_r, sb_r, cos_c, sa_c, sb_c)


UNROLL = 8


def _lambda_value(lv, lambda_init):
    a = jnp.sum(lv[0:1, :] * lv[1:2, :], axis=-1, keepdims=True)
    b = jnp.sum(lv[2:3, :] * lv[3:4, :], axis=-1, keepdims=True)
    return jnp.exp(a) - jnp.exp(b) + lambda_init


def _subln(o, g, lambda_init):
    ms = jnp.mean(o * o, axis=-1, keepdims=True)
    return o * lax.rsqrt(ms + LN_EPS) * g * (1.0 - lambda_init)


def _diff_attn_kernel(q_ref, qn_ref, k_ref, v_ref, kc_ref, vc_ref, lam_ref, g_ref, o_ref,
                      vt_ref, qt_ref, mx_ref, acc1_ref, acc2_ref, sa1_ref, sa2_ref, sb1_ref, sb2_ref,
                      *, tk, lambda_init):
    nk = k_ref.shape[0] // tk
    accs = (acc1_ref, acc2_ref)
    bufs = ((sa1_ref, sa2_ref), (sb1_ref, sb2_ref))
    first = pl.program_id(1) == 0
    unroll = math.gcd(nk, UNROLL)
    assert unroll % 2 == 0
    trips = nk // unroll

    def scores_into(i, dst, slot):
        off = pl.multiple_of(i * tk, tk)
        mx = []
        for c in range(2):
            cols = slice(c * HEAD_DIM, (c + 1) * HEAD_DIM)
            s = _dot(k_ref[pl.ds(off, tk), cols], qt_ref[slot, cols, :])
            dst[c][...] = s
            mx.append(jnp.max(s, axis=0, keepdims=True))
        return tuple(mx)

    @pl.when(first)
    def _():
        def fill(i, carry):
            off = pl.multiple_of(i * tk, tk)
            vt_ref[i] = v_ref[pl.ds(off, tk), :].astype(F32).T.astype(BF16)
            return carry
        lax.fori_loop(0, nk, fill, 0)
        qt_ref[0] = q_ref[...].astype(F32).T.astype(BF16)
        mx0 = scores_into(0, bufs[0], 0)
        mx_ref[0:1, :] = mx0[0]
        mx_ref[1:2, :] = mx0[1]

    @pl.when(jnp.logical_not(first))
    def _():
        qt_ref[0] = qt_ref[1]

    def softmax_pv(i, src, mx, carry):
        vt = vt_ref[i]
        out = []
        for c in range(2):
            if carry is None:
                mn = mx[c]
                p = jnp.exp2(src[c][...] - mn)
                l = jnp.sum(p, axis=0, keepdims=True)
                accs[c][...] = _dot(vt, p.astype(BF16))
            else:
                m, l = carry[2 * c], carry[2 * c + 1]
                mn = jnp.maximum(m, mx[c])
                a = jnp.exp2(m - mn)
                p = jnp.exp2(src[c][...] - mn)
                l = a * l + jnp.sum(p, axis=0, keepdims=True)
                accs[c][...] = a * accs[c][...] + _dot(vt, p.astype(BF16))
            out += [mn, l]
        return tuple(out)

    def context_step(carry):
        vc_t = vc_ref[...].astype(F32).T.astype(BF16)
        out = []
        for c in range(2):
            cols = slice(c * HEAD_DIM, (c + 1) * HEAD_DIM)
            m, l = carry[2 * c], carry[2 * c + 1]
            s = _dot(kc_ref[:, cols], qt_ref[0, cols, :])
            mn = jnp.maximum(m, jnp.max(s, axis=0, keepdims=True))
            a = jnp.exp2(m - mn)
            p = jnp.exp2(s - mn)
            l = a * l + jnp.sum(p, axis=0, keepdims=True)
            accs[c][...] = a * accs[c][...] + _dot(vc_t, p.astype(BF16))
            out += [mn, l]
        return tuple(out)

    def blocks(trip, mx, carry):
        base = trip * unroll
        for u in range(unroll):
            if u == unroll - 1:
                wrap = trip == trips - 1
                nxt = jnp.where(wrap, 0, base + unroll)
                mx_next = scores_into(nxt, bufs[(u + 1) % 2], jnp.where(wrap, 1, 0))
            else:
                mx_next = scores_into(base + u + 1, bufs[(u + 1) % 2], 0)
            start = carry is None
            carry = softmax_pv(base + u, bufs[u % 2], mx, carry)
            if start:
                carry = context_step(carry)
                eye = (lax.broadcasted_iota(jnp.int32, (2 * HEAD_DIM, 2 * HEAD_DIM), 0)
                       == lax.broadcasted_iota(jnp.int32, (2 * HEAD_DIM, 2 * HEAD_DIM), 1))
                qt_ref[1] = _dot_nt(jnp.where(eye, 1.0, 0.0).astype(BF16), qn_ref[...]).astype(BF16)
            mx = mx_next
        return mx, carry

    state = blocks(0, (mx_ref[0:1, :], mx_ref[1:2, :]), None)

    mx_n, (m1, l1, m2, l2) = lax.fori_loop(1, trips, lambda j, st: blocks(j, st[0], st[1]), state)
    mx_ref[0:1, :] = mx_n[0]
    mx_ref[1:2, :] = mx_n[1]

    lam = _lambda_value(lam_ref[...], lambda_init)
    o_t = acc1_ref[...] / l1 - lam * (acc2_ref[...] / l2)
    ms = jnp.mean(o_t * o_t, axis=0, keepdims=True)
    o = (o_t * lax.rsqrt(ms + LN_EPS)).T
    o_ref[...] = (o * g_ref[...] * (1.0 - lambda_init)).astype(BF16)


def _diff_attn(qkv, qkvc, lam_vec, subln_g, lambda_init, tq, tk):
    l = qkv.shape[0]
    lc = qkvc.shape[0]
    kern = functools.partial(_diff_attn_kernel, tk=tk, lambda_init=lambda_init)
    return pl.pallas_call(
        kern,
        grid=(DA_HEADS, l // tq),
        in_specs=[
            pl.BlockSpec((tq, 256), lambda h, i: (i, QD_COL // 2 + h)),
            pl.BlockSpec((tq, 256), lambda h, i: (jnp.minimum(i + 1, l // tq - 1), QD_COL // 2 + h)),
            pl.BlockSpec((l, 256), lambda h, i: (0, KD_COL // 2 + h)),
            pl.BlockSpec((l, 256), lambda h, i: (0, VD_COL // 2 + h)),
            pl.BlockSpec((lc, 256), lambda h, i: (0, KD_COL // 2 + h)),
            pl.BlockSpec((lc, 256), lambda h, i: (0, VD_COL // 2 + h)),
            pl.BlockSpec((4, HEAD_DIM), lambda h, i: (0, 0)),
            pl.BlockSpec((1, 2 * HEAD_DIM), lambda h, i: (0, 0)),
        ],
        out_specs=pl.BlockSpec((tq, 256), lambda h, i: (i, h)),
        out_shape=jax.ShapeDtypeStruct((l, DA_WIDTH), BF16),
        scratch_shapes=[pltpu.VMEM((l // tk, 256, tk), BF16),
                        pltpu.VMEM((2, 256, tq), BF16), pltpu.VMEM((8, tq), F32),
                        pltpu.VMEM((256, tq), F32), pltpu.VMEM((256, tq), F32)]
        + [pltpu.VMEM((tk, tq), F32)] * 4,
        compiler_params=_params(("parallel", "arbitrary")),
        name="diff_attn",
    )(qkv, qkv, qkv, qkv, qkvc, qkvc, lam_vec, subln_g)


NA_ROWS = 4
NA_TQ = NA_ROWS * GRID_W
NA_SLAB = NA_ROWS + NA_KH - 1
NA_SK = NA_SLAB * GRID_W


def _na_plan(l):
    rows = l // GRID_W
    kh = min(NA_KH, rows)
    assert kh == NA_KH and rows >= NA_SLAB and rows % NA_ROWS == 0
    nblk = rows // NA_ROWS
    starts, pats, geoms = [], [], []
    for b in range(nblk):
        r0 = b * NA_ROWS
        kstart = int(np.clip(r0 - kh // 2, 0, rows - NA_SLAB))
        rs = [int(np.clip(r0 + qr - kh // 2, 0, rows - kh)) for qr in range(NA_ROWS)]
        geom = (r0 - kstart, tuple(x - kstart for x in rs))
        assert all(0 <= x and x + kh <= NA_SLAB for x in geom[1])
        if geom not in geoms:
            geoms.append(geom)
        starts.append(kstart * GRID_W)
        pats.append(geoms.index(geom))
    return np.asarray(starts, np.int32), np.asarray(pats, np.int32), geoms


def _na_bias_kernel(t2_ref, o_ref, *, geoms):
    p = pl.program_id(1)
    lane = lax.broadcasted_iota(jnp.int32, (GRID_W, 2 * GRID_W), 1)
    for ps, (dq, rs) in enumerate(geoms):
        @pl.when(p == ps)
        def _(dq=dq, rs=rs):
            for qr in range(NA_ROWS):
                def inside(j, qr=qr):
                    return j < NA_SLAB and rs[qr] <= j < rs[qr] + NA_KH
                for j0 in range(0, NA_SLAB, 2):
                    left, right = inside(j0), inside(j0 + 1)
                    d0 = j0 - (dq + qr) + (NA_KH - 1)
                    if left or right:
                        blk = t2_ref[d0 + 1]
                        if not left:
                            blk = jnp.where(lane < GRID_W, MASK_VALUE, blk)
                        if not right:
                            blk = jnp.where(lane < GRID_W, blk, MASK_VALUE)
                    else:
                        blk = jnp.full((GRID_W, 2 * GRID_W), MASK_VALUE, F32)
                    width = min(2, NA_SLAB - j0) * GRID_W
                    o_ref[qr * GRID_W:(qr + 1) * GRID_W, j0 * GRID_W:j0 * GRID_W + width] = blk[:, :width]


def _na_bias_table(rpb, geoms):
    c = np.arange(GRID_W)
    dc = c[None, :] - c[:, None] + (NA_KW - 1)
    cs = np.clip(c - NA_KW // 2, 0, GRID_W - NA_KW)
    col_ok = (c[None, :] >= cs[:, None]) & (c[None, :] < cs[:, None] + NA_KW)
    e_col = np.zeros((2 * NA_KW - 1, GRID_W, GRID_W), np.float32)
    for ci in range(GRID_W):
        for ki in range(GRID_W):
            if col_ok[ci, ki]:
                e_col[dc[ci, ki], ci, ki] = 1.0
    t = jnp.einsum('hrd,dck->hrck', rpb.astype(F32), jnp.asarray(e_col), precision=lax.Precision.HIGHEST)
    t = jnp.where(jnp.asarray(col_ok)[None, None], t * LOG2E, MASK_VALUE)
    tp = jnp.pad(t, ((0, 0), (1, 1), (0, 0), (0, 0)))
    t2 = jnp.concatenate([tp[:, :-1], tp[:, 1:]], axis=-1)
    npat = len(geoms)
    return pl.pallas_call(
        functools.partial(_na_bias_kernel, geoms=tuple(geoms)),
        grid=(NA_HEADS, npat),
        in_specs=[pl.BlockSpec((None, 2 * NA_KH, GRID_W, 2 * GRID_W), lambda h, p: (h, 0, 0, 0))],
        out_specs=pl.BlockSpec((None, None, NA_TQ, NA_SK), lambda h, p: (p, h, 0, 0)),
        out_shape=jax.ShapeDtypeStruct((npat, NA_HEADS, NA_TQ, NA_SK), F32),
        compiler_params=_params(("parallel", "arbitrary")),
        name="na_bias",
    )(t2)


NA_HPS = 4


def _na_attn_kernel(start_ref, pat_ref, q_ref, k_ref, v_ref, kc_ref, vc_ref, bias_ref, o_ref):
    del pat_ref
    i = pl.program_id(1)
    off = pl.multiple_of(start_ref[i], GRID_W)
    for h in range(NA_HPS):
        cols = slice(h * HEAD_DIM, (h + 1) * HEAD_DIM)
        q = q_ref[:, cols]
        s_loc = _dot_nt(q, k_ref[pl.ds(off, NA_SK), cols]) + bias_ref[h]
        s_ctx = _dot_nt(q, kc_ref[:, cols])
        m = jnp.maximum(jnp.max(s_loc, axis=-1, keepdims=True), jnp.max(s_ctx, axis=-1, keepdims=True))
        p_loc = jnp.exp2(s_loc - m)
        p_ctx = jnp.exp2(s_ctx - m)
        l = jnp.sum(p_loc, axis=-1, keepdims=True) + jnp.sum(p_ctx, axis=-1, keepdims=True)
        o = (_dot(p_loc.astype(BF16), v_ref[pl.ds(off, NA_SK), cols])
             + _dot(p_ctx.astype(BF16), vc_ref[:, cols]))
        o_ref[:, cols] = (o / l).astype(BF16)


def _na_attn(qkv, qkvc, bias, starts, pats):
    l = qkv.shape[0]
    lc = qkvc.shape[0]
    w = NA_HPS * HEAD_DIM
    once = pl.Buffered(1)
    grid_spec = pltpu.PrefetchScalarGridSpec(
        num_scalar_prefetch=2,
        grid=(NA_HEADS // NA_HPS, l // NA_TQ),
        in_specs=[
            pl.BlockSpec((NA_TQ, w), lambda h, i, st, pt: (i, QN_COL // NA_HPS + h)),
            pl.BlockSpec((l, w), lambda h, i, st, pt: (0, KN_COL // NA_HPS + h), pipeline_mode=once),
            pl.BlockSpec((l, w), lambda h, i, st, pt: (0, VN_COL // NA_HPS + h), pipeline_mode=once),
            pl.BlockSpec((lc, w), lambda h, i, st, pt: (0, KN_COL // NA_HPS + h)),
            pl.BlockSpec((lc, w), lambda h, i, st, pt: (0, VN_COL // NA_HPS + h)),
            pl.BlockSpec((None, NA_HPS, NA_TQ, NA_SK), lambda h, i, st, pt: (pt[i], h, 0, 0)),
        ],
        out_specs=pl.BlockSpec((NA_TQ, w), lambda h, i, st, pt: (i, h)),
    )
    return pl.pallas_call(
        _na_attn_kernel,
        grid_spec=grid_spec,
        out_shape=jax.ShapeDtypeStruct((l, NA_WIDTH), BF16),
        compiler_params=_params(("parallel", "arbitrary")),
        name="na_attn",
    )(starts, pats, qkv, qkv, qkv, qkvc, qkvc, bias)


def _softmax_pv(q, k, v):
    s = _dot_nt(q, k)
    m = jnp.max(s, axis=-1, keepdims=True)
    p = jnp.exp2(s - m)
    l = jnp.sum(p, axis=-1, keepdims=True)
    return _dot(p.astype(BF16), v) / l


def _ctx_attn_kernel(x_ref, lam_ref, g_ref, od_ref, on_ref, *, lambda_init):
    lam = _lambda_value(lam_ref[...], lambda_init)
    g = g_ref[...]

    def col(base, h, width=HEAD_DIM):
        lo = (base + h) * HEAD_DIM
        return x_ref[:, lo:lo + width]

    for h in range(DA_HEADS):
        v = col(VD_COL, 2 * h, 2 * HEAD_DIM)
        o1 = _softmax_pv(col(QD_COL, 2 * h), col(KD_COL, 2 * h), v)
        o2 = _softmax_pv(col(QD_COL, 2 * h + 1), col(KD_COL, 2 * h + 1), v)
        od_ref[:, h * 256:(h + 1) * 256] = _subln(o1 - lam * o2, g, lambda_init).astype(BF16)
    for h in range(NA_HEADS):
        o = _softmax_pv(col(QN_COL, h), col(KN_COL, h), col(VN_COL, h))
        on_ref[:, h * HEAD_DIM:(h + 1) * HEAD_DIM] = o.astype(BF16)


def _ctx_attn(qkvc, lam_vec, subln_g, lambda_init):
    lc = qkvc.shape[0]
    kern = functools.partial(_ctx_attn_kernel, lambda_init=lambda_init)
    return pl.pallas_call(
        kern,
        grid=(1,),
        in_specs=[
            pl.BlockSpec((lc, IN_DIM), lambda i: (0, 0)),
            pl.BlockSpec((4, HEAD_DIM), lambda i: (0, 0)),
            pl.BlockSpec((1, 2 * HEAD_DIM), lambda i: (0, 0)),
        ],
        out_specs=[pl.BlockSpec((lc, DA_WIDTH), lambda i: (0, 0)),
                   pl.BlockSpec((lc, NA_WIDTH), lambda i: (0, 0))],
        out_shape=[jax.ShapeDtypeStruct((lc, DA_WIDTH), BF16),
                   jax.ShapeDtypeStruct((lc, NA_WIDTH), BF16)],
        compiler_params=_params(("arbitrary",)),
        name="ctx_attn",
    )(qkvc, lam_vec, subln_g)


OUT_SUB = 256


def _out_proj_kernel(od_ref, on_ref, w_ref, x_ref, gate_ref, g_ref, b_ref, o_ref):
    tm = x_ref.shape[0]
    for r0 in range(0, tm, OUT_SUB):
        rows = slice(r0, r0 + OUT_SUB)
        y = _dot(od_ref[rows, :], w_ref[0:DA_WIDTH, :]) + _dot(on_ref[rows, :], w_ref[DA_WIDTH:, :])
        z = ALPHA * x_ref[rows, :] + gate_ref[...] * y
        o_ref[rows, :] = _layernorm_rows(z, g_ref[...], b_ref[...])


def _out_proj_ln(od, on, w, layer, x, gate, ln_g, ln_b, tm):
    m = x.shape[0]
    vec = pl.BlockSpec((1, D_MODEL), lambda i: (0, 0))
    return pl.pallas_call(
        _out_proj_kernel,
        grid=(m // tm,),
        in_specs=[
            pl.BlockSpec((tm, DA_WIDTH), lambda i: (i, 0)),
            pl.BlockSpec((tm, NA_WIDTH), lambda i: (i, 0)),
            pl.BlockSpec((None, DA_WIDTH + NA_WIDTH, D_MODEL), lambda i: (layer, 0, 0),
                         pipeline_mode=pl.Buffered(1)),
            pl.BlockSpec((tm, D_MODEL), lambda i: (i, 0)),
            vec, vec, vec,
        ],
        out_specs=pl.BlockSpec((tm, D_MODEL), lambda i: (i, 0)),
        out_shape=jax.ShapeDtypeStruct((m, D_MODEL), F32),
        compiler_params=_params(("parallel",)),
        name="out_proj_ln",
    )(od, on, w, x, gate, ln_g, ln_b)


FFN_TF = 512
HALO = 16


def _ffn_kernel(x_ref, xp_ref, xn_ref, sc_ref, sh_ref, gate_ref, wg_ref, wu_ref, cw_ref, cb_ref, wd_ref,
                g_ref, b_ref, o_ref, h_ref, acc_ref):
    i = pl.program_id(0)
    c = pl.program_id(1)
    tm = x_ref.shape[0]
    n = tm + 2 * HALO

    @pl.when(c == 0)
    def _():
        sc = 1.0 + sc_ref[...]
        sh = sh_ref[...]
        before = jnp.where(i == 0, 0.0, xp_ref[...] * sc + sh)
        after = jnp.where(i == pl.num_programs(0) - 1, 0.0, xn_ref[...] * sc + sh)
        h_ref[0:HALO, :] = before.astype(BF16)
        h_ref[HALO:HALO + tm, :] = (x_ref[...] * sc + sh).astype(BF16)
        h_ref[HALO + tm:n, :] = after.astype(BF16)
        acc_ref[...] = jnp.zeros_like(acc_ref)

    g_ext = _dot(h_ref[...], wg_ref[...])
    u = _dot(h_ref[HALO:HALO + tm, :], wu_ref[...])
    g = g_ext[HALO:HALO + tm, :]
    g_prev = pltpu.roll(g_ext, 1, 0)[HALO:HALO + tm, :]
    g_next = pltpu.roll(g_ext, n - 1, 0)[HALO:HALO + tm, :]
    gc = g_prev * cw_ref[0:1, :] + g * cw_ref[1:2, :] + g_next * cw_ref[2:3, :] + cb_ref[...]
    act = gc / (1.0 + jnp.exp(-gc)) * u
    acc_ref[...] += _dot(act.astype(BF16), wd_ref[...])

    @pl.when(c == pl.num_programs(1) - 1)
    def _():
        z = ALPHA * x_ref[...] + gate_ref[...] * acc_ref[...]
        o_ref[...] = _layernorm_rows(z, g_ref[...], b_ref[...])


def _ffn_ln(x, sc, sh, gate, w_up, conv_w, conv_b, w_down, layer, ln_g, ln_b, tm):
    m = x.shape[0]
    nf = D_FF // FFN_TF
    tb = tm // HALO
    last_blk = m // HALO - 1
    vec = pl.BlockSpec((1, D_MODEL), lambda i, c: (0, 0))
    return pl.pallas_call(
        _ffn_kernel,
        grid=(m // tm, nf),
        in_specs=[
            pl.BlockSpec((tm, D_MODEL), lambda i, c: (i, 0)),
            pl.BlockSpec((HALO, D_MODEL), lambda i, c: (jnp.maximum(i * tb - 1, 0), 0)),
            pl.BlockSpec((HALO, D_MODEL), lambda i, c: (jnp.minimum((i + 1) * tb, last_blk), 0)),
            vec, vec, vec,
            pl.BlockSpec((None, D_MODEL, FFN_TF), lambda i, c: (layer, 0, c)),
            pl.BlockSpec((None, D_MODEL, FFN_TF), lambda i, c: (layer, 0, nf + c)),
            pl.BlockSpec((3, FFN_TF), lambda i, c: (0, c)),
            pl.BlockSpec((1, FFN_TF), lambda i, c: (0, c)),
            pl.BlockSpec((None, FFN_TF, D_MODEL), lambda i, c: (layer, c, 0)),
            vec, vec,
        ],
        out_specs=pl.BlockSpec((tm, D_MODEL), lambda i, c: (i, 0)),
        out_shape=jax.ShapeDtypeStruct((m, D_MODEL), F32),
        scratch_shapes=[pltpu.VMEM((tm + 2 * HALO, D_MODEL), BF16),
                        pltpu.VMEM((tm, D_MODEL), F32)],
        compiler_params=_params(("parallel", "arbitrary")),
        name="ffn_ln",
    )(x, x, x, sc, sh, gate, w_up, w_up, conv_w, conv_b, w_down, ln_g, ln_b)


def _rope_tables(l):
    half = HEAD_DIM // 2
    inv_freq = 1.0 / (ROPE_THETA ** (jnp.arange(0, half, 2, dtype=F32) / half))
    zeros = jnp.zeros((1, half), F32)

    def tables(pos, row_part):
        ang = pos.astype(F32)[:, None] * inv_freq[None, :]
        ang = jnp.concatenate([ang, ang], axis=-1)
        cos, sin = jnp.cos(ang), jnp.sin(ang)
        first = jnp.asarray(np.arange(half) < half // 2)[None, :]
        sin_a = jnp.where(first, -sin, 0.0)
        sin_b = jnp.where(first, 0.0, sin)
        pad = jnp.broadcast_to(zeros, cos.shape)
        order = (lambda t: jnp.concatenate([t, pad], axis=-1)) if row_part else \
                (lambda t: jnp.concatenate([pad, t], axis=-1))
        return order(cos), order(sin_a), order(sin_b)

    rows = tables(jnp.arange(l // GRID_W, dtype=jnp.int32), True)
    cols = tables(jnp.arange(GRID_W, dtype=jnp.int32), False)
    return rows + cols


def _identity_rope_tables(lc):
    nrow = lc // GRID_W
    lanes = jnp.asarray(np.arange(HEAD_DIM) < HEAD_DIM // 2, F32)[None, :]
    zr = jnp.zeros((nrow, HEAD_DIM), F32)
    zc = jnp.zeros((GRID_W, HEAD_DIM), F32)
    return (jnp.broadcast_to(lanes, (nrow, HEAD_DIM)), zr, zr,
            jnp.broadcast_to(1.0 - lanes, (GRID_W, HEAD_DIM)), zc, zc)


def kernel(x, c, ctx, c_ctx, w_ada, b_ada, w_in, da_lambda, da_subln, na_rpb, w_o,
           ln1_g, ln1_b, w_up, conv_w, conv_b, w_down, ln2_g, ln2_b):
    assert x.shape[0] == 1 and ctx.shape[0] == 1
    l = x.shape[1]
    lc = ctx.shape[1]
    xs = x[0]
    xc = ctx[0]

    cond = jnp.broadcast_to(jnp.stack([c[0], c_ctx])[:, :, None], (2, D_MODEL, 128))
    mods = _ada_mod(cond, w_ada, b_ada)

    rope = _rope_tables(l)
    rope_ctx = _identity_rope_tables(lc)
    w_in_b = w_in.astype(BF16)
    w_o_b = w_o.astype(BF16)
    w_up_b = w_up.astype(BF16)
    w_down_b = w_down.astype(BF16)
    starts, pats, geoms = _na_plan(l)
    starts = jnp.asarray(starts)
    pats = jnp.asarray(pats)

    for layer in range(DEPTH):
        last = layer == DEPTH - 1
        lambda_init = 0.8 - 0.6 * math.exp(-0.3 * layer)
        sh_a, sc_a, g_a, sh_m, sc_m, g_m = jnp.split(mods[layer, 0:1], N_MOD, axis=-1)
        shc_a, scc_a, gc_a, shc_m, scc_m, gc_m = jnp.split(mods[layer, 1:2], N_MOD, axis=-1)
        lam_vec = da_lambda[layer]
        subln_g = da_subln[layer].reshape(1, 2 * HEAD_DIM)
        ln1 = (ln1_g[layer].reshape(1, D_MODEL), ln1_b[layer].reshape(1, D_MODEL))
        ln2 = (ln2_g[layer].reshape(1, D_MODEL), ln2_b[layer].reshape(1, D_MODEL))
        cw = conv_w[layer]
        cb = conv_b[layer].reshape(1, D_FF)

        qkv = _in_proj(xs, sc_a, sh_a, w_in_b, layer, *rope, tm=256)
        qkvc = _in_proj(xc, scc_a, shc_a, w_in_b, layer, *rope_ctx, tm=lc)

        od = _diff_attn(qkv, qkvc, lam_vec, subln_g, lambda_init, tq=512, tk=512)
        bias = _na_bias_table(na_rpb[layer], geoms)
        on = _na_attn(qkv, qkvc, bias, starts, pats)

        xs = _out_proj_ln(od, on, w_o_b, layer, xs, g_a, *ln1, tm=512)
        xs = _ffn_ln(xs, sc_m, sh_m, g_m, w_up_b, cw, cb, w_down_b, layer, *ln2, tm=512)

        if not last:
            odc, onc = _ctx_attn(qkvc, lam_vec, subln_g, lambda_init)
            xc = _out_proj_ln(odc, onc, w_o_b, layer, xc, gc_a, *ln1, tm=lc)
            xc = _ffn_ln(xc, scc_m, shc_m, gc_m, w_up_b, cw, cb, w_down_b, layer, *ln2, tm=lc)

    return xs[None]
```
